```python
import math
import jax
import jax.numpy as jnp
from jax import lax
import numpy as np

D_MODEL = 4096
BATCH = 2
SEQ = 4096
DEPTH = 4
DEC_BATCH = 8
DEC_SEQ = 32
PAST_LEN = 1024

CHUNK = 64
N_EVEN = (DEPTH + 1) // 2
N_ODD = DEPTH // 2
NORM_EPS = 1e-6

LRU_WIDTH = D_MODEL // 2
LRU_HEADS = 16
LRU_BLOCK = LRU_WIDTH // LRU_HEADS
LRU_CONV = 4
LRU_C = 8.0

SSD_INNER = D_MODEL // 2
SSD_HEAD_DIM = 64
SSD_HEADS = SSD_INNER // SSD_HEAD_DIM
SSD_GROUPS = 8
SSD_STATE = 128
SSD_CONV = 4
SSD_BC = SSD_GROUPS * SSD_STATE
SSD_CONV_DIM = SSD_INNER + 2 * SSD_BC

OFF_LRU_X = 0
OFF_LRU_G = LRU_WIDTH
OFF_SSD_Z = 2 * LRU_WIDTH
OFF_SSD_XBC = OFF_SSD_Z + SSD_INNER
OFF_SSD_DT = OFF_SSD_XBC + SSD_CONV_DIM
AB_IN = OFF_SSD_DT + SSD_HEADS
AB_OUT = LRU_WIDTH + SSD_INNER

RET_HEADS = 16
RET_QK_DIM = D_MODEL // RET_HEADS
RET_V_DIM = 2 * RET_QK_DIM
RET_QK_WIDTH = RET_HEADS * RET_QK_DIM
RET_V_WIDTH = RET_HEADS * RET_V_DIM
RET_IN = 2 * RET_QK_WIDTH + 2 * RET_V_WIDTH
ROPE_BASE = 10000.0

D_FF = 3 * D_MODEL
FFN_CONV = 3

kernel_name = 'hybrid_streaming_encoder_step'


def rms_norm(x, w):
    x32 = x.astype(jnp.float32)
    y = x32 * lax.rsqrt(jnp.mean(x32 * x32, axis=-1, keepdims=True) + NORM_EPS)
    return (y * w.astype(jnp.float32)).astype(x.dtype)


def group_rms(y, groups):
    shp = y.shape
    yg = y.reshape(shp[:-1] + (groups, shp[-1] // groups))
    yg = yg * lax.rsqrt(jnp.mean(yg * yg, axis=-1, keepdims=True) + NORM_EPS)
    return yg.reshape(shp)


def causal_dwconv(x, hist, w, b):
    width = w.shape[0]
    length = x.shape[1]
    xp = jnp.concatenate([hist.astype(x.dtype), x], axis=1)
    y = b + xp[:, 0:length] * w[0]
    for k in range(1, width):
        y = y + xp[:, k:k + length] * w[k]
    return y, xp[:, -(width - 1):]


def rg_lru(x, h0, w_r, b_r, w_i, b_i, lam):
    bsz, length, width = x.shape
    xf = x.astype(jnp.float32)
    xb = xf.reshape(bsz, length, LRU_HEADS, LRU_BLOCK)
    r = jax.nn.sigmoid(jnp.einsum('blhi,hij->blhj', xb, w_r.astype(jnp.float32)).reshape(bsz, length, width) + b_r.astype(jnp.float32))
    i = jax.nn.sigmoid(jnp.einsum('blhi,hij->blhj', xb, w_i.astype(jnp.float32)).reshape(bsz, length, width) + b_i.astype(jnp.float32))
    log_a = -LRU_C * r * jax.nn.softplus(-lam.astype(jnp.float32))
    a = jnp.exp(log_a)
    u = jnp.sqrt(-jnp.expm1(2.0 * log_a)) * (i * xf)
    u = u.at[:, 0].add(a[:, 0] * h0.astype(jnp.float32))

    def combine(lhs, rhs):
        a1, u1 = lhs
        a2, u2 = rhs
        return a1 * a2, a2 * u1 + u2

    _, h = lax.associative_scan(combine, (a, u), axis=1)
    return h, h[:, -1]


def ssd_scan(x, dt, a, bm, cm, s0):
    bsz, length, nh, hp = x.shape
    g = SSD_GROUPS
    r = nh // g
    n = bm.shape[-1]
    t = min(CHUNK, length)
    nc = length // t
    xd = (x * dt[..., None]).reshape(bsz, nc, t, g, r, hp)
    adt = (a * dt).reshape(bsz, nc, t, g, r)
    bc = bm.reshape(bsz, nc, t, g, n)
    cc = cm.reshape(bsz, nc, t, g, n)
    acs = jnp.cumsum(adt, axis=2)
    diff = acs[:, :, :, None] - acs[:, :, None, :]
    mask = jnp.tril(jnp.ones((t, t), dtype=bool))[None, None, :, :, None, None]
    decay = jnp.exp(jnp.where(mask, diff, -jnp.inf))
    cb = jnp.einsum('bclgn,bcsgn->bclsg', cc, bc)
    y_diag = jnp.einsum('bclsg,bclsgr,bcsgrp->bclgrp', cb, decay, xd)
    dec_s = jnp.exp(acs[:, :, -1:] - acs)
    states = jnp.einsum('bcsgn,bcsgr,bcsgrp->bcgrpn', bc, dec_s, xd)
    chunk_dec = jnp.exp(acs[:, :, -1])

    def step(carry, inp):
        st, dc = inp
        return carry * dc[..., None, None] + st, carry

    init = s0.reshape(bsz, g, r, hp, n)
    final, prev = lax.scan(step, init, (states.swapaxes(0, 1), chunk_dec.swapaxes(0, 1)))
    prev = prev.swapaxes(0, 1)
    y_off = jnp.einsum('bclgn,bcgrpn,bclgr->bclgrp', cc, prev, jnp.exp(acs))
    y = (y_diag + y_off).reshape(bsz, length, nh, hp)
    return y, final.reshape(bsz, nh, hp, n)


def mixer_ab(h, lru_hist, lru_h0, ssd_hist, ssd_s0, w_in, lru_conv_w, lru_conv_b, lru_w_r, lru_b_r,
             lru_w_i, lru_b_i, lru_lambda, ssd_conv_w, ssd_conv_b, ssd_dt_bias, ssd_a_log, ssd_d,
             ssd_norm_w, w_out):
    dtype = h.dtype
    bsz, length, _ = h.shape
    proj = h @ w_in
    xa = proj[..., OFF_LRU_X:OFF_LRU_G]
    ga = proj[..., OFF_LRU_G:OFF_SSD_Z]
    zb = proj[..., OFF_SSD_Z:OFF_SSD_XBC]
    xbc = proj[..., OFF_SSD_XBC:OFF_SSD_DT]
    dtr = proj[..., OFF_SSD_DT:AB_IN]
    xa_c, lru_hist_new = causal_dwconv(xa, lru_hist, lru_conv_w, lru_conv_b)
    hseq, h_last = rg_lru(xa_c, lru_h0, lru_w_r, lru_b_r, lru_w_i, lru_b_i, lru_lambda)
    ya = hseq.astype(dtype) * jax.nn.gelu(ga, approximate=True)
    xbc_c, ssd_hist_new = causal_dwconv(xbc, ssd_hist, ssd_conv_w, ssd_conv_b)
    xbc_c = jax.nn.silu(xbc_c.astype(jnp.float32))
    xs = xbc_c[..., :SSD_INNER].reshape(bsz, length, SSD_HEADS, SSD_HEAD_DIM)
    bm = xbc_c[..., SSD_INNER:SSD_INNER + SSD_BC].reshape(bsz, length, SSD_GROUPS, SSD_STATE)
    cm = xbc_c[..., SSD_INNER + SSD_BC:].reshape(bsz, length, SSD_GROUPS, SSD_STATE)
    dt = jax.nn.softplus(dtr.astype(jnp.float32) + ssd_dt_bias.astype(jnp.float32))
    a = -jnp.exp(ssd_a_log.astype(jnp.float32))
    yb, s_new = ssd_scan(xs, dt, a, bm, cm, ssd_s0.astype(jnp.float32))
    yb = yb + ssd_d.astype(jnp.float32)[:, None] * xs
    yb = yb.reshape(bsz, length, SSD_INNER) * jax.nn.silu(zb.astype(jnp.float32))
    yb = group_rms(yb, SSD_GROUPS) * ssd_norm_w.astype(jnp.float32)
    mix = jnp.concatenate([ya, yb.astype(dtype)], axis=-1) @ w_out
    return mix, lru_hist_new, h_last.astype(dtype), ssd_hist_new, s_new.astype(dtype)


def rotary(x, pos):
    half = x.shape[-1] // 2
    inv = 1.0 / (ROPE_BASE ** jnp.linspace(0.0, 1.0, half, dtype=jnp.float32))
    ang = pos[:, None] * inv[None, :]
    cos = jnp.cos(ang)[None, :, None, :]
    sin = jnp.sin(ang)[None, :, None, :]
    x1, x2 = x[..., :half], x[..., half:]
    return jnp.concatenate([x1 * cos - x2 * sin, x2 * cos + x1 * sin], axis=-1)


def mixer_ret(h, s0, pos0, w_in, w_out):
    f32 = jnp.float32
    dtype = h.dtype
    bsz, length, _ = h.shape
    proj = h @ w_in
    q = proj[..., :RET_QK_WIDTH].astype(f32).reshape(bsz, length, RET_HEADS, RET_QK_DIM)
    k = proj[..., RET_QK_WIDTH:2 * RET_QK_WIDTH].astype(f32).reshape(bsz, length, RET_HEADS, RET_QK_DIM)
    v = proj[..., 2 * RET_QK_WIDTH:2 * RET_QK_WIDTH + RET_V_WIDTH].astype(f32).reshape(bsz, length, RET_HEADS, RET_V_DIM)
    g = proj[..., 2 * RET_QK_WIDTH + RET_V_WIDTH:]
    pos = float(pos0) + jnp.arange(length, dtype=f32)
    q = rotary(q, pos)
    k = rotary(k, pos) * (RET_QK_DIM ** -0.5)
    log_gamma = jnp.log1p(-jnp.exp2(-5.0 - jnp.arange(RET_HEADS, dtype=f32)))
    t = min(CHUNK, length)
    nc = length // t
    j = jnp.arange(t, dtype=f32)
    intra = jnp.exp(jnp.abs(j[:, None] - j[None, :])[None] * log_gamma[:, None, None])
    cross = jnp.exp((j[:, None] + 1.0) * log_gamma[None, :])
    kv_dec = jnp.exp((t - 1.0 - j)[:, None] * log_gamma[None, :])
    chunk_dec = jnp.exp(t * log_gamma)

    def to_chunks(z):
        return z.reshape((bsz, nc, t) + z.shape[2:]).swapaxes(0, 1)

    def step(s, inp):
        qc, kc, vc = inp
        att = jnp.einsum('bnhd,bmhd->bhnm', qc, kc) * intra
        o = jnp.einsum('bhnm,bmhe->bnhe', att, vc) + jnp.einsum('bnhd,bhde->bnhe', qc, s) * cross[None, :, :, None]
        s = s * chunk_dec[None, :, None, None] + jnp.einsum('bmhd,mh,bmhe->bhde', kc, kv_dec, vc)
        return s, o

    s_fin, o = lax.scan(step, s0.astype(f32), (to_chunks(q), to_chunks(k), to_chunks(v)))
    o = o.swapaxes(0, 1).reshape(bsz, length, RET_V_WIDTH)
    o = group_rms(o, RET_HEADS)
    y = (jax.nn.silu(g.astype(f32)) * o).astype(dtype)
    return y @ w_out, s_fin.astype(dtype)


def conv_ffn(h, hist, w_up, conv_w, conv_b, w_down):
    gu = h @ w_up
    gate, up = gu[..., :D_FF], gu[..., D_FF:]
    gate_c, hist_new = causal_dwconv(gate, hist, conv_w, conv_b)
    return (jax.nn.gelu(gate_c, approximate=True) * up) @ w_down, hist_new


def trunk(x, lru_conv, lru_h, ssd_conv, ssd_s, ret_s, ffn_conv, pos0, norms, ab_params, ret_params, ffn_params):
    n_mix_pre, n_mix_post, n_ffn_pre, n_ffn_post = norms
    new_lc, new_lh, new_sc, new_ss, new_rs, new_fc = [], [], [], [], [], []
    for layer in range(DEPTH):
        e = layer // 2
        hn = rms_norm(x, n_mix_pre[layer])
        if layer % 2 == 0:
            mix, lc, lh, sc, ss = mixer_ab(hn, lru_conv[e], lru_h[e], ssd_conv[e], ssd_s[e], *[p[e] for p in ab_params])
            new_lc.append(lc)
            new_lh.append(lh)
            new_sc.append(sc)
            new_ss.append(ss)
        else:
            mix, rs = mixer_ret(hn, ret_s[e], pos0, *[p[e] for p in ret_params])
            new_rs.append(rs)
        x = x + rms_norm(mix, n_mix_post[layer])
        hn = rms_norm(x, n_ffn_pre[layer])
        f, fc = conv_ffn(hn, ffn_conv[layer], *[p[layer] for p in ffn_params])
        new_fc.append(fc)
        x = x + rms_norm(f, n_ffn_post[layer])
    return (x, jnp.stack(new_lc), jnp.stack(new_lh), jnp.stack(new_sc), jnp.stack(new_ss),
            jnp.stack(new_rs), jnp.stack(new_fc))


def setup_inputs(seed: int = 0) -> dict:
    key = jax.random.key(seed)
    ks = list(jax.random.split(key, 40))
    f32 = jnp.float32

    def nrm(shape, scale):
        return scale * jax.random.normal(ks.pop(), shape, f32)

    def gain(shape):
        return 1.0 + nrm(shape, 0.02)

    x_prompt = nrm((BATCH, SEQ, D_MODEL), 1.0)
    x_sample = nrm((DEC_BATCH, DEC_SEQ, D_MODEL), 1.0)
    state_lru_conv = nrm((N_EVEN, DEC_BATCH, LRU_CONV - 1, LRU_WIDTH), 1.0)
    state_lru_h = nrm((N_EVEN, DEC_BATCH, LRU_WIDTH), 0.5)
    state_ssd_conv = nrm((N_EVEN, DEC_BATCH, SSD_CONV - 1, SSD_CONV_DIM), 1.0)
    state_ssd = nrm((N_EVEN, DEC_BATCH, SSD_HEADS, SSD_HEAD_DIM, SSD_STATE), 0.1)
    state_ret = nrm((N_ODD, DEC_BATCH, RET_HEADS, RET_QK_DIM, RET_V_DIM), 1.0)
    state_ffn_conv = nrm((DEPTH, DEC_BATCH, FFN_CONV - 1, D_FF), 1.0)
    norm_mix_pre = gain((DEPTH, D_MODEL))
    norm_mix_post = gain((DEPTH, D_MODEL))
    norm_ffn_pre = gain((DEPTH, D_MODEL))
    norm_ffn_post = gain((DEPTH, D_MODEL))
    w_in_ab = nrm((N_EVEN, D_MODEL, AB_IN), D_MODEL ** -0.5)
    lru_conv_w = nrm((N_EVEN, LRU_CONV, LRU_WIDTH), LRU_CONV ** -0.5)
    lru_conv_b = nrm((N_EVEN, LRU_WIDTH), 0.01)
    lru_w_r = nrm((N_EVEN, LRU_HEADS, LRU_BLOCK, LRU_BLOCK), LRU_BLOCK ** -0.5)
    lru_b_r = nrm((N_EVEN, LRU_WIDTH), 0.01)
    lru_w_i = nrm((N_EVEN, LRU_HEADS, LRU_BLOCK, LRU_BLOCK), LRU_BLOCK ** -0.5)
    lru_b_i = nrm((N_EVEN, LRU_WIDTH), 0.01)
    lam_u = jax.random.uniform(ks.pop(), (N_EVEN, LRU_WIDTH), f32, 0.9, 0.999)
    lam_s = lam_u ** (1.0 / LRU_C)
    lru_lambda = jnp.log(lam_s) - jnp.log1p(-lam_s)
    ssd_conv_w = nrm((N_EVEN, SSD_CONV, SSD_CONV_DIM), SSD_CONV ** -0.5)
    ssd_conv_b = nrm((N_EVEN, SSD_CONV_DIM), 0.01)
    dt0 = jnp.exp(jax.random.uniform(ks.pop(), (N_EVEN, SSD_HEADS), f32, math.log(1e-3), math.log(1e-1)))
    ssd_dt_bias = dt0 + jnp.log(-jnp.expm1(-dt0))
    ssd_a_log = jnp.log(jax.random.uniform(ks.pop(), (N_EVEN, SSD_HEADS), f32, 1.0, 16.0))
    ssd_d = gain((N_EVEN, SSD_HEADS))
    ssd_norm_w = gain((N_EVEN, SSD_INNER))
    w_out_ab = nrm((N_EVEN, AB_OUT, D_MODEL), AB_OUT ** -0.5)
    w_in_ret = nrm((N_ODD, D_MODEL, RET_IN), D_MODEL ** -0.5)
    w_out_ret = nrm((N_ODD, RET_V_WIDTH, D_MODEL), RET_V_WIDTH ** -0.5)
    ffn_w_up = nrm((DEPTH, D_MODEL, 2 * D_FF), D_MODEL ** -0.5)
    ffn_conv_w = nrm((DEPTH, FFN_CONV, D_FF), FFN_CONV ** -0.5)
    ffn_conv_b = nrm((DEPTH, D_FF), 0.01)
    ffn_w_down = nrm((DEPTH, D_FF, D_MODEL), D_FF ** -0.5)
    return {
        'x_prompt': x_prompt, 'x_sample': x_sample,
        'state_lru_conv': state_lru_conv, 'state_lru_h': state_lru_h,
        'state_ssd_conv': state_ssd_conv, 'state_ssd': state_ssd,
        'state_ret': state_ret, 'state_ffn_conv': state_ffn_conv,
        'norm_mix_pre': norm_mix_pre, 'norm_mix_post': norm_mix_post,
        'norm_ffn_pre': norm_ffn_pre, 'norm_ffn_post': norm_ffn_post,
        'w_in_ab': w_in_ab, 'lru_conv_w': lru_conv_w, 'lru_conv_b': lru_conv_b,
        'lru_w_r': lru_w_r, 'lru_b_r': lru_b_r, 'lru_w_i': lru_w_i, 'lru_b_i': lru_b_i,
        'lru_lambda': lru_lambda, 'ssd_conv_w': ssd_conv_w, 'ssd_conv_b': ssd_conv_b,
        'ssd_dt_bias': ssd_dt_bias, 'ssd_a_log': ssd_a_log, 'ssd_d': ssd_d,
        'ssd_norm_w': ssd_norm_w, 'w_out_ab': w_out_ab,
        'w_in_ret': w_in_ret, 'w_out_ret': w_out_ret,
        'ffn_w_up': ffn_w_up, 'ffn_conv_w': ffn_conv_w, 'ffn_conv_b': ffn_conv_b,
        'ffn_w_down': ffn_w_down,
    }


def reference(x_prompt, x_sample, state_lru_conv, state_lru_h, state_ssd_conv, state_ssd, state_ret,
              state_ffn_conv, norm_mix_pre, norm_mix_post, norm_ffn_pre, norm_ffn_post, w_in_ab,
              lru_conv_w, lru_conv_b, lru_w_r, lru_b_r, lru_w_i, lru_b_i, lru_lambda, ssd_conv_w,
              ssd_conv_b, ssd_dt_bias, ssd_a_log, ssd_d, ssd_norm_w, w_out_ab, w_in_ret, w_out_ret,
              ffn_w_up, ffn_conv_w, ffn_conv_b, ffn_w_down):
    norms = (norm_mix_pre, norm_mix_post, norm_ffn_pre, norm_ffn_post)
    ab_params = (w_in_ab, lru_conv_w, lru_conv_b, lru_w_r, lru_b_r, lru_w_i, lru_b_i, lru_lambda,
                 ssd_conv_w, ssd_conv_b, ssd_dt_bias, ssd_a_log, ssd_d, ssd_norm_w, w_out_ab)
    ret_params = (w_in_ret, w_out_ret)
    ffn_params = (ffn_w_up, ffn_conv_w, ffn_conv_b, ffn_w_down)
    bp = x_prompt.shape[0]
    dtp = x_prompt.dtype
    (y_prompt, p_lru_conv, p_lru_h, p_ssd_conv, p_ssd, p_ret, p_ffn_conv) = trunk(
        x_prompt,
        jnp.zeros((N_EVEN, bp, LRU_CONV - 1, LRU_WIDTH), dtp),
        jnp.zeros((N_EVEN, bp, LRU_WIDTH), dtp),
        jnp.zeros((N_EVEN, bp, SSD_CONV - 1, SSD_CONV_DIM), dtp),
        jnp.zeros((N_EVEN, bp, SSD_HEADS, SSD_HEAD_DIM, SSD_STATE), dtp),
        jnp.zeros((N_ODD, bp, RET_HEADS, RET_QK_DIM, RET_V_DIM), dtp),
        jnp.zeros((DEPTH, bp, FFN_CONV - 1, D_FF), dtp),
        0, norms, ab_params, ret_params, ffn_params)
    (y_sample, s_lru_conv, s_lru_h, s_ssd_conv, s_ssd, s_ret, s_ffn_conv) = trunk(
        x_sample, state_lru_conv, state_lru_h, state_ssd_conv, state_ssd, state_ret, state_ffn_conv,
        PAST_LEN, norms, ab_params, ret_params, ffn_params)
    return (y_prompt, y_sample, p_lru_conv, p_lru_h, p_ssd_conv, p_ssd, p_ret, p_ffn_conv,
            s_lru_conv, s_lru_h, s_ssd_conv, s_ssd, s_ret, s_ffn_conv)
```

```python
import functools

import jax
import jax.numpy as jnp
from jax import lax
from jax.experimental import pallas as pl
from jax.experimental.pallas import tpu as pltpu

F32 = jnp.float32
BF16 = jnp.bfloat16

D_MODEL = 4096
PAST_LEN = 1024
CHUNK = 64
NORM_EPS = 1e-6
LRU_WIDTH = D_MODEL // 2
LRU_HEADS = 16
LRU_BLOCK = LRU_WIDTH // LRU_HEADS
LRU_CONV = 4
LRU_C = 8.0
SSD_INNER = D_MODEL // 2
SSD_HEAD_DIM = 64
SSD_HEADS = SSD_INNER // SSD_HEAD_DIM
SSD_GROUPS = 8
SSD_STATE = 128
SSD_CONV = 4
SSD_BC = SSD_GROUPS * SSD_STATE
SSD_CONV_DIM = SSD_INNER + 2 * SSD_BC
OFF_SSD_Z = 2 * LRU_WIDTH
OFF_SSD_XBC = OFF_SSD_Z + SSD_INNER
OFF_SSD_DT = OFF_SSD_XBC + SSD_CONV_DIM
RET_HEADS = 16
RET_QK_DIM = D_MODEL // RET_HEADS
RET_V_DIM = 2 * RET_QK_DIM
RET_QK_WIDTH = RET_HEADS * RET_QK_DIM
RET_V_WIDTH = RET_HEADS * RET_V_DIM
ROPE_BASE = 10000.0
D_FF = 3 * D_MODEL
FFN_CONV = 3

SUBLANES = 8
LANES = 128
BF16_ROWS = 16
HIST_ROWS = SUBLANES
MIB = 1 << 20
VMEM_LIMIT_BIG = 56 * MIB
VMEM_LIMIT_SMALL = 40 * MIB

HEADS_PER_GROUP = SSD_HEADS // SSD_GROUPS
GROUP_WIDTH = HEADS_PER_GROUP * SSD_HEAD_DIM
LRU_TILE = 512


def _pick_tile(total, cap, mult):
    best = None
    for t in range(mult, min(total, cap) + 1, mult):
        if total % t == 0:
            best = t
    assert best is not None, (total, cap, mult)
    return best


def _silu(x):
    return x * jax.nn.sigmoid(x)


def _gelu(x):
    return jax.nn.gelu(x, approximate=True)


def _softplus(x):
    return jnp.maximum(x, 0.0) + jnp.log1p(jnp.exp(-jnp.abs(x)))


def _rms(x, w):
    return x * lax.rsqrt(jnp.mean(x * x, axis=-1, keepdims=True) + NORM_EPS) * w


def _norm_kernel(x_ref, w_ref, hn_ref):
    hn_ref[...] = _rms(x_ref[...], w_ref[...]).astype(BF16)


def _resid_norm_kernel(x_ref, m_ref, wp_ref, wn_ref, xo_ref, hn_ref):
    xn = x_ref[...] + _rms(m_ref[...], wp_ref[...])
    xo_ref[...] = xn
    hn_ref[...] = _rms(xn, wn_ref[...]).astype(BF16)


def _resid_kernel(x_ref, m_ref, wp_ref, xo_ref):
    xo_ref[...] = x_ref[...] + _rms(m_ref[...], wp_ref[...])


def _row_call(kernel_fn, arrays, weights, out_dtypes):
    m, d = arrays[0].shape
    tr = _pick_tile(m, 256, BF16_ROWS)
    row_spec = pl.BlockSpec((tr, d), lambda i: (i, 0))
    w_spec = pl.BlockSpec((1, d), lambda i: (0, 0))
    outs = tuple(jax.ShapeDtypeStruct((m, d), dt) for dt in out_dtypes)
    res = pl.pallas_call(
        kernel_fn,
        out_shape=outs,
        grid=(m // tr,),
        in_specs=[row_spec] * len(arrays) + [w_spec] * len(weights),
        out_specs=tuple(row_spec for _ in outs),
        compiler_params=pltpu.CompilerParams(dimension_semantics=("arbitrary",), vmem_limit_bytes=VMEM_LIMIT_SMALL),
    )(*arrays, *[w.reshape(1, d) for w in weights])
    return res


def _matmul_kernel(a_ref, w_ref, o_ref, wb_ref, *acc, nk):
    kk = pl.program_id(1)
    i = pl.program_id(2)

    @pl.when(i == 0)
    def _():
        wb_ref[...] = w_ref[...].astype(BF16)

    p = jnp.dot(a_ref[...], wb_ref[...], preferred_element_type=F32)
    if nk == 1:
        o_ref[...] = p.astype(o_ref.dtype)
    else:
        (acc_ref,) = acc
        tm = a_ref.shape[0]
        rows = pl.ds(pl.multiple_of(i * tm, SUBLANES), tm)

        @pl.when(kk == 0)
        def _():
            acc_ref[rows, :] = p

        @pl.when(jnp.logical_and(kk > 0, kk < nk - 1))
        def _():
            acc_ref[rows, :] += p

        @pl.when(kk == nk - 1)
        def _():
            o_ref[...] = (acc_ref[rows, :] + p).astype(o_ref.dtype)


def _matmul(a, w, layer, *, n_cols=None, out_dtype=F32, tn=512):
    m, k = a.shape
    n = n_cols if n_cols is not None else w.shape[-1]
    tm = _pick_tile(m, 1056, BF16_ROWS)
    tk = k if k <= 4096 else 2048
    nk = k // tk
    assert n % tn == 0 and k % tk == 0
    if nk == 1:
        out_map = lambda j, kk, i: (i, j)
    else:
        out_map = lambda j, kk, i: (jnp.where(kk == nk - 1, i, 0), j)
    scratch = [pltpu.VMEM((tk, tn), BF16)]
    if nk > 1:
        scratch.append(pltpu.VMEM((m, tn), F32))
    return pl.pallas_call(
        functools.partial(_matmul_kernel, nk=nk),
        out_shape=jax.ShapeDtypeStruct((m, n), out_dtype),
        grid=(n // tn, nk, m // tm),
        in_specs=[
            pl.BlockSpec((tm, tk), lambda j, kk, i: (i, kk)),
            pl.BlockSpec((None, tk, tn), lambda j, kk, i: (layer, kk, j)),
        ],
        out_specs=pl.BlockSpec((tm, tn), out_map),
        scratch_shapes=scratch,
        compiler_params=pltpu.CompilerParams(
            dimension_semantics=("arbitrary", "arbitrary", "arbitrary"), vmem_limit_bytes=VMEM_LIMIT_BIG),
    )(a, w)


def _conv_from_buf(buf_ref, cw_ref, cb_ref, width, rows):
    base = HIST_ROWS - (width - 1)
    y = cb_ref[...] + buf_ref[base:base + rows, :] * cw_ref[0:1, :]
    for k in range(1, width):
        y = y + buf_ref[base + k:base + k + rows, :] * cw_ref[k:k + 1, :]
    return y


def _conv_step(buf_ref, src, hist_ref, cw_ref, cb_ref, width, t, rows):
    @pl.when(t == 0)
    def _():
        buf_ref[0:HIST_ROWS, :] = hist_ref[0]

    @pl.when(t > 0)
    def _():
        buf_ref[0:HIST_ROWS, :] = buf_ref[rows:rows + HIST_ROWS, :]

    buf_ref[HIST_ROWS:HIST_ROWS + rows, :] = src
    return _conv_from_buf(buf_ref, cw_ref, cb_ref, width, rows)


def _ffn_up_kernel(a_ref, wg_ref, wu_ref, cw_ref, cb_ref, hist_ref, h_ref, st_ref, wb_ref, buf_ref, *,
                   tiles_per_seq, n_ptiles, n_pseq, n_sseq, ls, tf):
    i = pl.program_id(1)
    tm = a_ref.shape[0]

    @pl.when(i == 0)
    def _():
        wb_ref[:, :tf] = wg_ref[...].astype(BF16)
        wb_ref[:, tf:] = wu_ref[...].astype(BF16)

    @pl.when(i < n_ptiles)
    def _():
        p = jnp.dot(a_ref[...], wb_ref[...], preferred_element_type=F32)
        seq = i // tiles_per_seq
        pos = i % tiles_per_seq

        @pl.when(pos == 0)
        def _():
            buf_ref[0:HIST_ROWS, :] = hist_ref[seq]

        @pl.when(pos > 0)
        def _():
            buf_ref[0:HIST_ROWS, :] = buf_ref[tm:tm + HIST_ROWS, :]

        buf_ref[HIST_ROWS:HIST_ROWS + tm, :] = p[:, :tf]
        gate = _gelu(_conv_from_buf(buf_ref, cw_ref, cb_ref, FFN_CONV, tm))
        h_ref[...] = (gate * p[:, tf:]).astype(BF16)

        @pl.when(pos == tiles_per_seq - 1)
        def _():
            st_ref[seq] = buf_ref[tm:tm + HIST_ROWS, :]

    @pl.when(i == n_ptiles)
    def _():
        rows = n_sseq * ls
        p = jnp.dot(a_ref[:rows, :], wb_ref[...], preferred_element_type=F32)
        for s in range(n_sseq):
            lo = s * ls
            buf_ref[0:HIST_ROWS, :] = hist_ref[n_pseq + s]
            buf_ref[HIST_ROWS:HIST_ROWS + ls, :] = p[lo:lo + ls, :tf]
            gate = _gelu(_conv_from_buf(buf_ref, cw_ref, cb_ref, FFN_CONV, ls))
            h_ref[lo:lo + ls, :] = (gate * p[lo:lo + ls, tf:]).astype(BF16)
            st_ref[n_pseq + s] = buf_ref[ls:ls + HIST_ROWS, :]


def _ffn_up(hn, w_up, conv_w, conv_b, hist, layer, *, bp, lp, bs, ls, tf=256):
    m, d = hn.shape
    tm = min(1024, lp)
    assert lp % tm == 0 and bs * ls <= tm and ls % BF16_ROWS == 0
    n_ptiles = bp * lp // tm
    nseq = bp + bs
    nf = D_FF // tf
    kern = functools.partial(_ffn_up_kernel, tiles_per_seq=lp // tm, n_ptiles=n_ptiles, n_pseq=bp, n_sseq=bs, ls=ls, tf=tf)
    return pl.pallas_call(
        kern,
        out_shape=(jax.ShapeDtypeStruct((m, D_FF), BF16), jax.ShapeDtypeStruct((nseq, HIST_ROWS, D_FF), F32)),
        grid=(nf, n_ptiles + 1),
        in_specs=[
            pl.BlockSpec((tm, d), lambda j, i: (i, 0)),
            pl.BlockSpec((None, d, tf), lambda j, i: (layer, 0, j)),
            pl.BlockSpec((None, d, tf), lambda j, i: (layer, 0, nf + j)),
            pl.BlockSpec((None, FFN_CONV, tf), lambda j, i: (layer, 0, j)),
            pl.BlockSpec((None, 1, tf), lambda j, i: (layer, 0, j)),
            pl.BlockSpec((nseq, HIST_ROWS, tf), lambda j, i: (0, 0, j)),
        ],
        out_specs=(
            pl.BlockSpec((tm, tf), lambda j, i: (i, j)),
            pl.BlockSpec((nseq, HIST_ROWS, tf), lambda j, i: (0, 0, j)),
        ),
        scratch_shapes=[pltpu.VMEM((d, 2 * tf), BF16), pltpu.VMEM((tm + HIST_ROWS, tf), F32)],
        compiler_params=pltpu.CompilerParams(dimension_semantics=("arbitrary", "arbitrary"), vmem_limit_bytes=VMEM_LIMIT_BIG),
    )(hn, w_up, w_up, conv_w, conv_b.reshape(conv_b.shape[0], 1, D_FF), hist)


def _lru_kernel(*refs, T, nt, aliased):
    (xa_ref, ga_ref, hist_ref, h0_ref, cw_ref, cb_ref, wr_ref, br_ref, wi_ref, bi_ref, lam_ref) = refs[:11]
    rest = refs[12:] if aliased else refs[11:]
    y_ref, cst_ref, hl_ref, buf_ref, hc_ref = rest
    t = pl.program_id(2)
    tc = xa_ref.shape[1]

    @pl.when(t == 0)
    def _():
        hc_ref[...] = h0_ref[0]

    xc = _conv_step(buf_ref, xa_ref[...], hist_ref, cw_ref, cb_ref, LRU_CONV, t, T)
    xcb = xc.astype(BF16)
    rs, gs = [], []
    for hh in range(tc // LRU_BLOCK):
        xb = xcb[:, hh * LRU_BLOCK:(hh + 1) * LRU_BLOCK]
        rs.append(jnp.dot(xb, wr_ref[hh].astype(BF16), preferred_element_type=F32))
        gs.append(jnp.dot(xb, wi_ref[hh].astype(BF16), preferred_element_type=F32))
    r = jax.nn.sigmoid(jnp.concatenate(rs, axis=-1) + br_ref[...])
    ig = jax.nn.sigmoid(jnp.concatenate(gs, axis=-1) + bi_ref[...])
    log_a = -LRU_C * r * _softplus(-lam_ref[...])
    a = jnp.exp(log_a)
    u = jnp.sqrt(-(jnp.tanh(log_a) * (jnp.exp(2.0 * log_a) + 1.0))) * (ig * xc)

    row = lax.broadcasted_iota(jnp.int32, (T, tc), 0)
    s = 1
    while s < T:
        keep = row >= s
        u = u + a * jnp.where(keep, pltpu.roll(u, s, 0), 0.0)
        a = a * jnp.where(keep, pltpu.roll(a, s, 0), 1.0)
        s *= 2
    h = u + a * hc_ref[...]
    hc_ref[...] = h[T - 1:T, :]
    y_ref[...] = (h * _gelu(ga_ref[...])).astype(BF16)

    @pl.when(t == nt - 1)
    def _():
        cst_ref[0] = buf_ref[T:T + HIST_ROWS, :]
        hl_ref[0] = h[T - 1:T, :]


def _lru_call(proj, y_prev, hist, h0, p, e, *, n_seq, L, T, row0, m):
    nt = L // T
    rb0 = row0 // T
    tc = LRU_TILE
    nc = LRU_WIDTH // tc
    hpt = tc // LRU_BLOCK
    row = lambda s, c, t: rb0 + s * nt + t
    vec = lambda: pl.BlockSpec((None, 1, tc), lambda s, c, t: (e, 0, c))
    in_specs = [
        pl.BlockSpec((T, tc), lambda s, c, t: (row(s, c, t), c)),
        pl.BlockSpec((T, tc), lambda s, c, t: (row(s, c, t), nc + c)),
        pl.BlockSpec((1, HIST_ROWS, tc), lambda s, c, t: (s, 0, c)),
        pl.BlockSpec((1, 1, tc), lambda s, c, t: (s, 0, c)),
        pl.BlockSpec((None, LRU_CONV, tc), lambda s, c, t: (e, 0, c)),
        vec(),
        pl.BlockSpec((None, hpt, LRU_BLOCK, LRU_BLOCK), lambda s, c, t: (e, c, 0, 0)),
        vec(),
        pl.BlockSpec((None, hpt, LRU_BLOCK, LRU_BLOCK), lambda s, c, t: (e, c, 0, 0)),
        vec(),
        vec(),
    ]
    args = [proj, proj, hist, h0, p["lru_conv_w"], p["lru_conv_b"], p["lru_w_r"], p["lru_b_r"], p["lru_w_i"], p["lru_b_i"],
            p["lru_lambda"]]
    aliases = {}
    if y_prev is not None:
        in_specs.append(pl.BlockSpec(memory_space=pl.ANY))
        args.append(y_prev)
        aliases = {len(args) - 1: 0}
    return pl.pallas_call(
        functools.partial(_lru_kernel, T=T, nt=nt, aliased=y_prev is not None),
        out_shape=(jax.ShapeDtypeStruct((m, D_MODEL), BF16),
                   jax.ShapeDtypeStruct((n_seq, HIST_ROWS, LRU_WIDTH), F32),
                   jax.ShapeDtypeStruct((n_seq, 1, LRU_WIDTH), F32)),
        grid=(n_seq, nc, nt),
        in_specs=in_specs,
        out_specs=(pl.BlockSpec((T, tc), lambda s, c, t: (row(s, c, t), c)),
                   pl.BlockSpec((1, HIST_ROWS, tc), lambda s, c, t: (s, 0, c)),
                   pl.BlockSpec((1, 1, tc), lambda s, c, t: (s, 0, c))),
        scratch_shapes=[pltpu.VMEM((T + HIST_ROWS, tc), F32), pltpu.VMEM((1, tc), F32)],
        input_output_aliases=aliases,
        compiler_params=pltpu.CompilerParams(dimension_semantics=("arbitrary", "arbitrary", "arbitrary"),
                                             vmem_limit_bytes=VMEM_LIMIT_SMALL),
    )(*args)


def _ssd_kernel(*refs, T, nt):
    (z_ref, x_ref, b_ref, c_ref, dt_ref, hx_ref, hb_ref, hcn_ref, s0_ref,
     cwx_ref, cbx_ref, cwb_ref, cbb_ref, cwc_ref, cbc_ref, dtb_ref, alog_ref, dd_ref, nw_ref, _y_prev,
     y_ref, csx_ref, csb_ref, csc_ref, sf_ref, bufx_ref, bufb_ref, bufc_ref, s_ref, yd_ref) = refs
    t = pl.program_id(2)
    P = SSD_HEAD_DIM

    xs = _silu(_conv_step(bufx_ref, x_ref[...], hx_ref, cwx_ref, cbx_ref, SSD_CONV, t, T))
    bm = _silu(_conv_step(bufb_ref, b_ref[...], hb_ref, cwb_ref, cbb_ref, SSD_CONV, t, T))
    cm = _silu(_conv_step(bufc_ref, c_ref[...], hcn_ref, cwc_ref, cbc_ref, SSD_CONV, t, T))

    @pl.when(t == 0)
    def _():
        s_ref[...] = s0_ref[0, 0]

    dt = _softplus(dt_ref[...] + dtb_ref[...])
    adt = -jnp.exp(alog_ref[...]) * dt
    row = lax.broadcasted_iota(jnp.int32, (T, GROUP_WIDTH), 0)
    acs = adt
    s = 1
    while s < T:
        acs = acs + jnp.where(row >= s, pltpu.roll(acs, s, 0), 0.0)
        s *= 2
    xd = xs * dt
    bmb = bm.astype(BF16)
    cmb = cm.astype(BF16)
    cb = lax.dot_general(cmb, bmb, (((1,), (1,)), ((), ())), preferred_element_type=F32)
    tri = lax.broadcasted_iota(jnp.int32, (T, T), 0) >= lax.broadcasted_iota(jnp.int32, (T, T), 1)
    for half in range(GROUP_WIDTH // LANES):
        blk = acs[:, half * LANES:(half + 1) * LANES]
        if T < LANES:
            blk = jnp.concatenate([blk, jnp.zeros((LANES - T, LANES), F32)], axis=0)
        blk_t = blk.T
        for rr in range(LANES // P):
            r = half * (LANES // P) + rr
            col = acs[:, r * P:r * P + 1]
            rowv = blk_t[rr * P:rr * P + 1, :T]
            decay = jnp.exp(jnp.where(tri, col - rowv, -jnp.inf))
            md = (cb * decay).astype(BF16)
            yd_ref[:, r * P:(r + 1) * P] = jnp.dot(md, xd[:, r * P:(r + 1) * P].astype(BF16), preferred_element_type=F32)

    st = s_ref[...]
    y_off = jnp.dot(cmb, st.astype(BF16), preferred_element_type=F32) * jnp.exp(acs)
    last = acs[T - 1:T, :]
    dec_s = jnp.exp(last - acs)
    st_new = st * jnp.exp(last) + lax.dot_general(bmb, (xd * dec_s).astype(BF16), (((0,), (0,)), ((), ())),
                                                  preferred_element_type=F32)
    s_ref[...] = st_new
    yt = yd_ref[...] + y_off + dd_ref[...] * xs
    yt = yt * _silu(z_ref[...])
    yt = yt * lax.rsqrt(jnp.mean(yt * yt, axis=-1, keepdims=True) + NORM_EPS) * nw_ref[...]
    y_ref[...] = yt.astype(BF16)

    @pl.when(t == nt - 1)
    def _():
        csx_ref[0] = bufx_ref[T:T + HIST_ROWS, :]
        csb_ref[0] = bufb_ref[T:T + HIST_ROWS, :]
        csc_ref[0] = bufc_ref[T:T + HIST_ROWS, :]
        sf_ref[0, 0] = st_new


def _ssd_call(proj, dtr, y_prev, hist, s0, p, e, *, n_seq, L, T, row0, m):
    nt = L // T
    rb0 = row0 // T
    G = SSD_GROUPS
    gw = GROUP_WIDTH
    n = SSD_STATE
    row = lambda s, g, t: rb0 + s * nt + t
    zb = OFF_SSD_Z // gw
    xb = OFF_SSD_XBC // gw
    bb = (OFF_SSD_XBC + SSD_INNER) // n
    cb = (OFF_SSD_XBC + SSD_INNER + SSD_BC) // n
    hb = SSD_INNER // n
    hc = (SSD_INNER + SSD_BC) // n
    vec = lambda: pl.BlockSpec((None, 1, gw), lambda s, g, t: (e, 0, g))
    in_specs = [
        pl.BlockSpec((T, gw), lambda s, g, t: (row(s, g, t), zb + g)),
        pl.BlockSpec((T, gw), lambda s, g, t: (row(s, g, t), xb + g)),
        pl.BlockSpec((T, n), lambda s, g, t: (row(s, g, t), bb + g)),
        pl.BlockSpec((T, n), lambda s, g, t: (row(s, g, t), cb + g)),
        pl.BlockSpec((T, gw), lambda s, g, t: (row(s, g, t), g)),
        pl.BlockSpec((1, HIST_ROWS, gw), lambda s, g, t: (s, 0, g)),
        pl.BlockSpec((1, HIST_ROWS, n), lambda s, g, t: (s, 0, hb + g)),
        pl.BlockSpec((1, HIST_ROWS, n), lambda s, g, t: (s, 0, hc + g)),
        pl.BlockSpec((1, 1, n, gw), lambda s, g, t: (s, g, 0, 0)),
        pl.BlockSpec((None, SSD_CONV, gw), lambda s, g, t: (e, 0, g)),
        pl.BlockSpec((None, 1, gw), lambda s, g, t: (e, 0, g)),
        pl.BlockSpec((None, SSD_CONV, n), lambda s, g, t: (e, 0, hb + g)),
        pl.BlockSpec((None, 1, n), lambda s, g, t: (e, 0, hb + g)),
        pl.BlockSpec((None, SSD_CONV, n), lambda s, g, t: (e, 0, hc + g)),
        pl.BlockSpec((None, 1, n), lambda s, g, t: (e, 0, hc + g)),
        vec(), vec(), vec(), vec(),
        pl.BlockSpec(memory_space=pl.ANY),
    ]
    args = [proj, proj, proj, proj, dtr, hist, hist, hist, s0,
            p["ssd_conv_w"], p["ssd_conv_b"], p["ssd_conv_w"], p["ssd_conv_b"], p["ssd_conv_w"], p["ssd_conv_b"],
            p["ssd_dt_bias_x"], p["ssd_a_log_x"], p["ssd_d_x"], p["ssd_norm_w"], y_prev]
    yb = LRU_WIDTH // gw
    return pl.pallas_call(
        functools.partial(_ssd_kernel, T=T, nt=nt),
        out_shape=(jax.ShapeDtypeStruct((m, D_MODEL), BF16),
                   jax.ShapeDtypeStruct((n_seq, HIST_ROWS, SSD_INNER), F32),
                   jax.ShapeDtypeStruct((n_seq, HIST_ROWS, SSD_BC), F32),
                   jax.ShapeDtypeStruct((n_seq, HIST_ROWS, SSD_BC), F32),
                   jax.ShapeDtypeStruct((n_seq, G, n, gw), F32)),
        grid=(n_seq, G, nt),
        in_specs=in_specs,
        out_specs=(pl.BlockSpec((T, gw), lambda s, g, t: (row(s, g, t), yb + g)),
                   pl.BlockSpec((1, HIST_ROWS, gw), lambda s, g, t: (s, 0, g)),
                   pl.BlockSpec((1, HIST_ROWS, n), lambda s, g, t: (s, 0, g)),
                   pl.BlockSpec((1, HIST_ROWS, n), lambda s, g, t: (s, 0, g)),
                   pl.BlockSpec((1, 1, n, gw), lambda s, g, t: (s, g, 0, 0))),
        scratch_shapes=[pltpu.VMEM((T + HIST_ROWS, gw), F32), pltpu.VMEM((T + HIST_ROWS, n), F32),
                        pltpu.VMEM((T + HIST_ROWS, n), F32), pltpu.VMEM((n, gw), F32), pltpu.VMEM((T, gw), F32)],
        input_output_aliases={len(args) - 1: 0},
        compiler_params=pltpu.CompilerParams(dimension_semantics=("arbitrary", "arbitrary", "arbitrary"),
                                             vmem_limit_bytes=VMEM_LIMIT_SMALL),
    )(*args)


def _ret_kernel(*refs, nt, aliased):
    (q_ref, k_ref, v_ref, g_ref, cos_ref, sin_ref, dm_ref, cr_ref, kd_ref, cd_ref, s0_ref) = refs[:11]
    rest = refs[12:] if aliased else refs[11:]
    y_ref, sf_ref, s_ref = rest
    t = pl.program_id(2)
    half = RET_QK_DIM // 2

    @pl.when(t == 0)
    def _():
        s_ref[...] = s0_ref[0, 0]

    cos = cos_ref[...]
    sin = sin_ref[...]

    def rot(x):
        x1 = x[:, :half]
        x2 = x[:, half:]
        return jnp.concatenate([x1 * cos - x2 * sin, x2 * cos + x1 * sin], axis=-1)

    q = rot(q_ref[...])
    k = rot(k_ref[...]) * (RET_QK_DIM ** -0.5)
    qb = q.astype(BF16)
    vb = v_ref[...].astype(BF16)
    att = lax.dot_general(qb, k.astype(BF16), (((1,), (1,)), ((), ())), preferred_element_type=F32) * dm_ref[0]
    st = s_ref[...]
    o = (jnp.dot(att.astype(BF16), vb, preferred_element_type=F32)
         + jnp.dot((q * cr_ref[0]).astype(BF16), st.astype(BF16), preferred_element_type=F32))
    st_new = st * cd_ref[0][0:1, :] + lax.dot_general((k * kd_ref[0]).astype(BF16), vb, (((0,), (0,)), ((), ())),
                                                       preferred_element_type=F32)
    s_ref[...] = st_new
    o = o * lax.rsqrt(jnp.mean(o * o, axis=-1, keepdims=True) + NORM_EPS)
    y_ref[...] = (_silu(g_ref[...]) * o).astype(BF16)

    @pl.when(t == nt - 1)
    def _():
        sf_ref[0, 0] = st_new


def _ret_tables(T, ct, pos0, L):
    f32 = F32
    log_gamma = jnp.log1p(-jnp.exp2(-5.0 - jnp.arange(RET_HEADS, dtype=f32)))
    n = jnp.arange(T, dtype=f32)
    chunk = jnp.arange(T) // ct
    same = chunk[:, None] == chunk[None, :]
    earlier = chunk[None, :] < chunk[:, None]
    diff = n[:, None] - n[None, :]
    lg = log_gamma[:, None, None]
    dm = jnp.where(same[None], jnp.exp(jnp.abs(diff)[None] * lg), jnp.where(earlier[None], jnp.exp(diff[None] * lg), 0.0))
    cross = jnp.exp((n[None, :] + 1.0) * log_gamma[:, None])
    kvd = jnp.exp((T - 1.0 - n)[None, :] * log_gamma[:, None])
    cdec = jnp.exp(T * log_gamma)
    cr = jnp.broadcast_to(cross[:, :, None], (RET_HEADS, T, RET_QK_DIM))
    kd = jnp.broadcast_to(kvd[:, :, None], (RET_HEADS, T, RET_QK_DIM))
    cd = jnp.broadcast_to(cdec[:, None, None], (RET_HEADS, SUBLANES, RET_V_DIM))
    half = RET_QK_DIM // 2
    inv = 1.0 / (ROPE_BASE ** jnp.linspace(0.0, 1.0, half, dtype=f32))
    pos = float(pos0) + jnp.arange(L, dtype=f32)
    ang = pos[:, None] * inv[None, :]
    return dm, cr, kd, cd, jnp.cos(ang), jnp.sin(ang)


def _ret_call(proj, y_prev, s0, *, n_seq, L, T, row0, pos0, m):
    nt = L // T
    rb0 = row0 // T
    ct = min(CHUNK, L)
    assert T % ct == 0
    dm, cr, kd, cd, cos, sin = _ret_tables(T, ct, pos0, L)
    H = RET_HEADS
    dk, dv = RET_QK_DIM, RET_V_DIM
    row = lambda s, h, t: rb0 + s * nt + t
    kb0 = RET_QK_WIDTH // dk
    vb0 = 2 * RET_QK_WIDTH // dv
    gb0 = (2 * RET_QK_WIDTH + RET_V_WIDTH) // dv
    in_specs = [
        pl.BlockSpec((T, dk), lambda s, h, t: (row(s, h, t), h)),
        pl.BlockSpec((T, dk), lambda s, h, t: (row(s, h, t), kb0 + h)),
        pl.BlockSpec((T, dv), lambda s, h, t: (row(s, h, t), vb0 + h)),
        pl.BlockSpec((T, dv), lambda s, h, t: (row(s, h, t), gb0 + h)),
        pl.BlockSpec((T, dk // 2), lambda s, h, t: (t, 0)),
        pl.BlockSpec((T, dk // 2), lambda s, h, t: (t, 0)),
        pl.BlockSpec((1, T, T), lambda s, h, t: (h, 0, 0)),
        pl.BlockSpec((1, T, dk), lambda s, h, t: (h, 0, 0)),
        pl.BlockSpec((1, T, dk), lambda s, h, t: (h, 0, 0)),
        pl.BlockSpec((1, SUBLANES, dv), lambda s, h, t: (h, 0, 0)),
        pl.BlockSpec((1, 1, dk, dv), lambda s, h, t: (s, h, 0, 0)),
    ]
    args = [proj, proj, proj, proj, cos, sin, dm, cr, kd, cd, s0]
    aliases = {}
    if y_prev is not None:
        in_specs.append(pl.BlockSpec(memory_space=pl.ANY))
        args.append(y_prev)
        aliases = {len(args) - 1: 0}
    return pl.pallas_call(
        functools.partial(_ret_kernel, nt=nt, aliased=y_prev is not None),
        out_shape=(jax.ShapeDtypeStruct((m, RET_V_WIDTH), BF16), jax.ShapeDtypeStruct((n_seq, H, dk, dv), F32)),
        grid=(n_seq, H, nt),
        in_specs=in_specs,
        out_specs=(pl.BlockSpec((T, dv), lambda s, h, t: (row(s, h, t), h)),
                   pl.BlockSpec((1, 1, dk, dv), lambda s, h, t: (s, h, 0, 0))),
        scratch_shapes=[pltpu.VMEM((dk, dv), F32)],
        input_output_aliases=aliases,
        compiler_params=pltpu.CompilerParams(dimension_semantics=("arbitrary", "arbitrary", "arbitrary"),
                                             vmem_limit_bytes=VMEM_LIMIT_SMALL),
    )(*args)


def _pad_hist(state, n_zero_seq):
    n, km1, c = state.shape
    padded = jnp.pad(state, ((n_zero_seq, 0), (HIST_ROWS - km1, 0), (0, 0)))
    return padded


def _ssd_state_in(s):
    n = s.shape[0]
    s = s.reshape(n, SSD_GROUPS, HEADS_PER_GROUP, SSD_HEAD_DIM, SSD_STATE)
    return s.transpose(0, 1, 4, 2, 3).reshape(n, SSD_GROUPS, SSD_STATE, GROUP_WIDTH)


def _ssd_state_out(s):
    n = s.shape[0]
    s = s.reshape(n, SSD_GROUPS, SSD_STATE, HEADS_PER_GROUP, SSD_HEAD_DIM)
    return s.transpose(0, 1, 3, 4, 2).reshape(n, SSD_HEADS, SSD_HEAD_DIM, SSD_STATE)


def kernel(x_prompt, x_sample, state_lru_conv, state_lru_h, state_ssd_conv, state_ssd, state_ret, state_ffn_conv,
           norm_mix_pre, norm_mix_post, norm_ffn_pre, norm_ffn_post, w_in_ab, lru_conv_w, lru_conv_b, lru_w_r, lru_b_r,
           lru_w_i, lru_b_i, lru_lambda, ssd_conv_w, ssd_conv_b, ssd_dt_bias, ssd_a_log, ssd_d, ssd_norm_w, w_out_ab,
           w_in_ret, w_out_ret, ffn_w_up, ffn_conv_w, ffn_conv_b, ffn_w_down):
    bp, lp, d = x_prompt.shape
    bs, ls, _ = x_sample.shape
    depth = norm_mix_pre.shape[0]
    n_even = w_in_ab.shape[0]
    mp = bp * lp
    m = mp + bs * ls
    x = jnp.concatenate([x_prompt.reshape(mp, d), x_sample.reshape(bs * ls, d)], axis=0)

    groups = ((bp, lp, 0, 0), (bs, ls, mp, PAST_LEN))
    t_lru = (min(256, lp), ls)
    t_ssd = (min(128, lp), ls)
    t_ret = (min(256, lp), ls)

    rep = lambda v: jnp.repeat(v, SSD_HEAD_DIM, axis=-1).reshape(n_even, 1, SSD_INNER)
    vec3 = lambda v: v.reshape(v.shape[0], 1, v.shape[-1])
    ab = {
        "lru_conv_w": lru_conv_w, "lru_conv_b": vec3(lru_conv_b), "lru_w_r": lru_w_r, "lru_b_r": vec3(lru_b_r),
        "lru_w_i": lru_w_i, "lru_b_i": vec3(lru_b_i), "lru_lambda": vec3(lru_lambda),
        "ssd_conv_w": ssd_conv_w, "ssd_conv_b": vec3(ssd_conv_b),
        "ssd_dt_bias_x": rep(ssd_dt_bias), "ssd_a_log_x": rep(ssd_a_log), "ssd_d_x": rep(ssd_d),
        "ssd_norm_w": vec3(ssd_norm_w),
    }
    w_dt = jnp.repeat(w_in_ab[:, :, OFF_SSD_DT:], SSD_HEAD_DIM, axis=-1)

    ffn_hist = _pad_hist(state_ffn_conv.reshape(depth * bs, FFN_CONV - 1, D_FF), 0).reshape(depth, bs, HIST_ROWS, D_FF)
    ffn_hist = jnp.pad(ffn_hist, ((0, 0), (bp, 0), (0, 0), (0, 0)))

    new = {k: ([], []) for k in ("lc", "lh", "sc", "ss", "rs", "fc")}

    (hn,) = _row_call(_norm_kernel, [x], [norm_mix_pre[0]], [BF16])
    for layer in range(depth):
        e = layer // 2
        if layer % 2 == 0:
            proj = _matmul(hn, w_in_ab, e, n_cols=OFF_SSD_DT)
            dtr = _matmul(hn, w_dt, e)
            y = None
            for gi, (n_seq, L, row0, _) in enumerate(groups):
                if gi == 0:
                    hist = jnp.zeros((n_seq, HIST_ROWS, LRU_WIDTH), F32)
                    h0 = jnp.zeros((n_seq, 1, LRU_WIDTH), F32)
                else:
                    hist = _pad_hist(state_lru_conv[e], 0)
                    h0 = state_lru_h[e].reshape(n_seq, 1, LRU_WIDTH)
                y, cst, hl = _lru_call(proj, y, hist, h0, ab, e, n_seq=n_seq, L=L, T=t_lru[gi], row0=row0, m=m)
                new["lc"][gi].append(cst[:, HIST_ROWS - (LRU_CONV - 1):, :])
                new["lh"][gi].append(hl.reshape(n_seq, LRU_WIDTH))
            for gi, (n_seq, L, row0, _) in enumerate(groups):
                if gi == 0:
                    hist = jnp.zeros((n_seq, HIST_ROWS, SSD_CONV_DIM), F32)
                    s0 = jnp.zeros((n_seq, SSD_GROUPS, SSD_STATE, GROUP_WIDTH), F32)
                else:
                    hist = _pad_hist(state_ssd_conv[e], 0)
                    s0 = _ssd_state_in(state_ssd[e])
                y, csx, csb, csc, sf = _ssd_call(proj, dtr, y, hist, s0, ab, e, n_seq=n_seq, L=L, T=t_ssd[gi], row0=row0, m=m)
                cst = jnp.concatenate([csx, csb, csc], axis=-1)
                new["sc"][gi].append(cst[:, HIST_ROWS - (SSD_CONV - 1):, :])
                new["ss"][gi].append(_ssd_state_out(sf))
            mix = _matmul(y, w_out_ab, e)
        else:
            proj = _matmul(hn, w_in_ret, e)
            y = None
            for gi, (n_seq, L, row0, pos0) in enumerate(groups):
                if gi == 0:
                    s0 = jnp.zeros((n_seq, RET_HEADS, RET_QK_DIM, RET_V_DIM), F32)
                else:
                    s0 = state_ret[e]
                y, sf = _ret_call(proj, y, s0, n_seq=n_seq, L=L, T=t_ret[gi], row0=row0, pos0=pos0, m=m)
                new["rs"][gi].append(sf)
            mix = _matmul(y, w_out_ret, e)
        x, hn = _row_call(_resid_norm_kernel, [x, mix], [norm_mix_post[layer], norm_ffn_pre[layer]], [F32, BF16])
        h, fst = _ffn_up(hn, ffn_w_up, ffn_conv_w, ffn_conv_b, ffn_hist[layer], layer, bp=bp, lp=lp, bs=bs, ls=ls)
        fst = fst[:, HIST_ROWS - (FFN_CONV - 1):, :]
        new["fc"][0].append(fst[:bp])
        new["fc"][1].append(fst[bp:])
        f = _matmul(h, ffn_w_down, layer)
        if layer + 1 < depth:
            x, hn = _row_call(_resid_norm_kernel, [x, f], [norm_ffn_post[layer], norm_mix_pre[layer + 1]], [F32, BF16])
        else:
            (x,) = _row_call(_resid_kernel, [x, f], [norm_ffn_post[layer]], [F32])

    y_prompt = x[:mp].reshape(bp, lp, d)
    y_sample = x[mp:].reshape(bs, ls, d)
    outs = [y_prompt, y_sample]
    for gi in range(2):
        for k in ("lc", "lh", "sc", "ss", "rs", "fc"):
            outs.append(jnp.stack(new[k][gi]))
    return tuple(outs)
```

```python
import functools

import jax
import jax.numpy as jnp
from jax import lax
from jax.experimental import pallas as pl
from jax.experimental.pallas import tpu as pltpu

F32 = jnp.float32
BF16 = jnp.bfloat16

D_MODEL = 4096
PAST_LEN = 1024
CHUNK = 64
NORM_EPS = 1e-6
LRU_WIDTH = D_MODEL // 2
LRU_HEADS = 16
LRU_BLOCK = LRU_WIDTH // LRU_HEADS
LRU_CONV = 4
LRU_C = 8.0
SSD_INNER = D_MODEL // 2
SSD_HEAD_DIM = 64
SSD_HEADS = SSD_INNER // SSD_HEAD_DIM
SSD_GROUPS = 8
SSD_STATE = 128
SSD_CONV = 4
SSD_BC = SSD_GROUPS * SSD_STATE
SSD_CONV_DIM = SSD_INNER + 2 * SSD_BC
OFF_SSD_Z = 2 * LRU_WIDTH
OFF_SSD_XBC = OFF_SSD_Z + SSD_INNER
OFF_SSD_DT = OFF_SSD_XBC + SSD_CONV_DIM
RET_HEADS = 16
RET_QK_DIM = D_MODEL // RET_HEADS
RET_V_DIM = 2 * RET_QK_DIM
RET_QK_WIDTH = RET_HEADS * RET_QK_DIM
RET_V_WIDTH = RET_HEADS * RET_V_DIM
ROPE_BASE = 10000.0
D_FF = 3 * D_MODEL
FFN_CONV = 3

SUBLANES = 8
LANES = 128
BF16_ROWS = 16
HIST_ROWS = SUBLANES
MIB = 1 << 20
VMEM_LIMIT_BIG = 56 * MIB
VMEM_LIMIT_SMALL = 40 * MIB

HEADS_PER_GROUP = SSD_HEADS // SSD_GROUPS
GROUP_WIDTH = HEADS_PER_GROUP * SSD_HEAD_DIM
LRU_TILE = 512
MATMUL_TM_CAP = 1056
MATMUL_TK = 4096
FFN_TILE = 1024
FFN_COLS = 512
RET_HEADS_PER_STEP = 2
SSD_GROUPS_PER_STEP = 2


def _pick_tile(total, cap, mult):
    best = None
    for t in range(mult, min(total, cap) + 1, mult):
        if total % t == 0:
            best = t
    assert best is not None, (total, cap, mult)
    return best


def _silu(x):
    return x * jax.nn.sigmoid(x)


def _gelu(x):
    return jax.nn.gelu(x, approximate=True)


def _softplus(x):
    return jnp.maximum(x, 0.0) + jnp.log1p(jnp.exp(-jnp.abs(x)))


def _rms(x, w):
    return x * lax.rsqrt(jnp.mean(x * x, axis=-1, keepdims=True) + NORM_EPS) * w


def _resid_norm_kernel(x_ref, m_ref, wp_ref, wn_ref, xo_ref, hn_ref):
    xn = x_ref[...] + _rms(m_ref[...], wp_ref[...])
    xo_ref[...] = xn
    hn_ref[...] = _rms(xn, wn_ref[...]).astype(BF16)


def _first_norm_kernel(xp_ref, xs_ref, w_ref, xo_ref, hn_ref, *, n_ptiles):
    i = pl.program_id(0)

    def emit(x):
        xo_ref[...] = x
        hn_ref[...] = _rms(x, w_ref[...]).astype(BF16)

    @pl.when(i < n_ptiles)
    def _():
        emit(xp_ref[...])

    @pl.when(i >= n_ptiles)
    def _():
        emit(xs_ref[...])


def _last_resid_kernel(x_ref, m_ref, wp_ref, yp_ref, ys_ref, *, n_ptiles):
    i = pl.program_id(0)
    xn = x_ref[...] + _rms(m_ref[...], wp_ref[...])

    @pl.when(i < n_ptiles)
    def _():
        yp_ref[...] = xn

    @pl.when(i >= n_ptiles)
    def _():
        ys_ref[...] = xn


def _row_tile(mp, ms):
    tr = _pick_tile(mp, 256, BF16_ROWS)
    assert ms % tr == 0
    return tr


def _first_norm(xp, xs, w):
    mp, d = xp.shape
    ms = xs.shape[0]
    tr = _row_tile(mp, ms)
    n_ptiles = mp // tr
    row_spec = pl.BlockSpec((tr, d), lambda i: (i, 0))
    return pl.pallas_call(
        functools.partial(_first_norm_kernel, n_ptiles=n_ptiles),
        name="first_norm",
        out_shape=(jax.ShapeDtypeStruct((mp + ms, d), F32), jax.ShapeDtypeStruct((mp + ms, d), BF16)),
        grid=((mp + ms) // tr,),
        in_specs=[pl.BlockSpec((tr, d), lambda i: (jnp.minimum(i, n_ptiles - 1), 0)),
                  pl.BlockSpec((tr, d), lambda i: (jnp.maximum(i - n_ptiles, 0), 0)),
                  pl.BlockSpec((1, d), lambda i: (0, 0))],
        out_specs=(row_spec, row_spec),
        compiler_params=pltpu.CompilerParams(dimension_semantics=("arbitrary",), vmem_limit_bytes=VMEM_LIMIT_SMALL),
    )(xp, xs, w.reshape(1, d))


def _last_resid(x, mix, w, mp):
    m, d = x.shape
    ms = m - mp
    tr = _row_tile(mp, ms)
    n_ptiles = mp // tr
    row_spec = pl.BlockSpec((tr, d), lambda i: (i, 0))
    return pl.pallas_call(
        functools.partial(_last_resid_kernel, n_ptiles=n_ptiles),
        name="last_resid",
        out_shape=(jax.ShapeDtypeStruct((mp, d), F32), jax.ShapeDtypeStruct((ms, d), F32)),
        grid=(m // tr,),
        in_specs=[row_spec, row_spec, pl.BlockSpec((1, d), lambda i: (0, 0))],
        out_specs=(pl.BlockSpec((tr, d), lambda i: (jnp.minimum(i, n_ptiles - 1), 0)),
                   pl.BlockSpec((tr, d), lambda i: (jnp.maximum(i - n_ptiles, 0), 0))),
        compiler_params=pltpu.CompilerParams(dimension_semantics=("arbitrary",), vmem_limit_bytes=VMEM_LIMIT_SMALL),
    )(x, mix, w.reshape(1, d))


def _row_call(kernel_fn, arrays, weights, out_dtypes):
    m, d = arrays[0].shape
    tr = _pick_tile(m, 256, BF16_ROWS)
    row_spec = pl.BlockSpec((tr, d), lambda i: (i, 0))
    w_spec = pl.BlockSpec((1, d), lambda i: (0, 0))
    outs = tuple(jax.ShapeDtypeStruct((m, d), dt) for dt in out_dtypes)
    res = pl.pallas_call(
        kernel_fn,
        name=kernel_fn.__name__.strip("_"),
        out_shape=outs,
        grid=(m // tr,),
        in_specs=[row_spec] * len(arrays) + [w_spec] * len(weights),
        out_specs=tuple(row_spec for _ in outs),
        compiler_params=pltpu.CompilerParams(dimension_semantics=("arbitrary",), vmem_limit_bytes=VMEM_LIMIT_SMALL),
    )(*arrays, *[w.reshape(1, d) for w in weights])
    return res


def _matmul_kernel(a_ref, w_ref, o_ref, wb0_ref, wb1_ref, *acc, nk, nq, ck, w_transposed):
    q = pl.program_id(0)
    i = pl.program_id(1)
    tm = a_ref.shape[0]
    wb_refs = (wb0_ref, wb1_ref)

    def stage(par):
        w = w_ref[...]
        if w_transposed:
            w = w.T
        wb_refs[par][pl.ds(pl.multiple_of(i * ck, BF16_ROWS), ck), :] = w.astype(BF16)

    def product(par):
        p = jnp.dot(a_ref[...], wb_refs[par][...], preferred_element_type=F32)
        if nk == 1:
            o_ref[...] = p.astype(o_ref.dtype)
        else:
            (acc_ref,) = acc
            kk = (q - 1) % nk
            rows = pl.ds(pl.multiple_of(i * tm, SUBLANES), tm)

            @pl.when(kk == 0)
            def _():
                acc_ref[rows, :] = p

            @pl.when(jnp.logical_and(kk > 0, kk < nk - 1))
            def _():
                acc_ref[rows, :] += p

            @pl.when(kk == nk - 1)
            def _():
                o_ref[...] = (acc_ref[rows, :] + p).astype(o_ref.dtype)

    @pl.when(q == 0)
    def _():
        stage(0)

    for par in (0, 1):
        @pl.when(jnp.logical_and(jnp.logical_and(q >= 1, q < nq), q % 2 == par))
        def _():
            stage(par)
            product(1 - par)

    @pl.when(q == nq)
    def _():
        product((nq - 1) % 2)


def _matmul(a, w, layer, *, n_cols=None, out_dtype=F32, w_transposed=False, tag=""):
    m, k = a.shape
    n = n_cols if n_cols is not None else w.shape[-2 if w_transposed else -1]
    tm = _pick_tile(m, MATMUL_TM_CAP, BF16_ROWS)
    n_mt = m // tm
    tk = min(k, MATMUL_TK)
    nk = k // tk
    tn = 1024 if nk == 1 else 512
    ck = tk // n_mt
    assert n % tn == 0 and k % tk == 0 and tk % n_mt == 0 and ck % LANES == 0
    nq = (n // tn) * nk

    def a_map(q, i):
        p = jnp.maximum(q - 1, 0)
        return (jnp.where(q >= 1, i, 0), p % nk)

    def w_map(q, i):
        p = jnp.minimum(q, nq - 1)
        row = jnp.where(q < nq, (p % nk) * n_mt + i, (nk - 1) * n_mt + n_mt - 1)
        return (layer, p // nk, row) if w_transposed else (layer, row, p // nk)

    def o_map(q, i):
        p = jnp.maximum(q - 1, 0)
        return (jnp.where(jnp.logical_and(q >= 1, p % nk == nk - 1), i, 0), p // nk)

    scratch = [pltpu.VMEM((tk, tn), BF16), pltpu.VMEM((tk, tn), BF16)]
    if nk > 1:
        scratch.append(pltpu.VMEM((m, tn), F32))
    return pl.pallas_call(
        functools.partial(_matmul_kernel, nk=nk, nq=nq, ck=ck, w_transposed=w_transposed),
        name=f"matmul_k{k}_n{n}{tag}",
        out_shape=jax.ShapeDtypeStruct((m, n), out_dtype),
        grid=(nq + 1, n_mt),
        in_specs=[
            pl.BlockSpec((tm, tk), a_map),
            pl.BlockSpec((None, tn, ck) if w_transposed else (None, ck, tn), w_map),
        ],
        out_specs=pl.BlockSpec((tm, tn), o_map),
        scratch_shapes=scratch,
        compiler_params=pltpu.CompilerParams(
            dimension_semantics=("arbitrary", "arbitrary"), vmem_limit_bytes=VMEM_LIMIT_BIG),
    )(a, w)


def _dt_kernel(hn_ref, w_ref, o_ref):
    tm = hn_ref.shape[0]
    dt = lax.dot_general(hn_ref[...], w_ref[...].astype(BF16), (((1,), (1,)), ((), ())),
                         preferred_element_type=F32)
    for h in range(SSD_HEADS):
        o_ref[:, h * SSD_HEAD_DIM:(h + 1) * SSD_HEAD_DIM] = jnp.broadcast_to(dt[:, h:h + 1], (tm, SSD_HEAD_DIM))


def _dt_proj(hn, w_t, layer):
    m, d = hn.shape
    tm = _pick_tile(m, MATMUL_TM_CAP, BF16_ROWS)
    assert OFF_SSD_DT % SSD_HEADS == 0
    return pl.pallas_call(
        _dt_kernel,
        name="dt_proj",
        out_shape=jax.ShapeDtypeStruct((m, SSD_INNER), F32),
        grid=(m // tm,),
        in_specs=[pl.BlockSpec((tm, d), lambda i: (i, 0)),
                  pl.BlockSpec((None, SSD_HEADS, d), lambda i: (layer, OFF_SSD_DT // SSD_HEADS, 0))],
        out_specs=pl.BlockSpec((tm, SSD_INNER), lambda i: (i, 0)),
        compiler_params=pltpu.CompilerParams(dimension_semantics=("arbitrary",), vmem_limit_bytes=VMEM_LIMIT_SMALL),
    )(hn, w_t)


def _conv_from_buf(buf_ref, cw_ref, cb_ref, width, rows):
    base = HIST_ROWS - (width - 1)
    y = cb_ref[...] + buf_ref[base:base + rows, :] * cw_ref[0:1, :]
    for k in range(1, width):
        y = y + buf_ref[base + k:base + k + rows, :] * cw_ref[k:k + 1, :]
    return y


def _conv_step(buf_ref, src, hist_ref, cw_ref, cb_ref, width, t, rows):
    @pl.when(t == 0)
    def _():
        buf_ref[0:HIST_ROWS, :] = hist_ref[0]

    @pl.when(t > 0)
    def _():
        buf_ref[0:HIST_ROWS, :] = buf_ref[rows:rows + HIST_ROWS, :]

    buf_ref[HIST_ROWS:HIST_ROWS + rows, :] = src
    return _conv_from_buf(buf_ref, cw_ref, cb_ref, width, rows)


def _ffn_up_kernel(a_ref, wg_ref, wu_ref, cw_ref, cb_ref, hist_ref, h_ref, st_ref, wb0_ref, wb1_ref, buf_ref, *,
                   tiles_per_seq, n_ptiles, n_pseq, n_sseq, ls, tf, nq):
    q = pl.program_id(0)
    i = pl.program_id(1)
    tm = a_ref.shape[0]
    ck = wg_ref.shape[0]
    wb_refs = (wb0_ref, wb1_ref)
    H = HIST_ROWS

    def stage(par):
        rows = pl.ds(pl.multiple_of(i * ck, BF16_ROWS), ck)
        wb_refs[par][rows, 0:tf] = wg_ref[...].astype(BF16)
        wb_refs[par][rows, tf:2 * tf] = wu_ref[...].astype(BF16)

    def prompt_tile(par):
        p = jnp.dot(a_ref[...], wb_refs[par][...], preferred_element_type=F32)
        seq = i // tiles_per_seq
        first = (i % tiles_per_seq) == 0
        buf_ref[0:H, :] = jnp.where(first, hist_ref[seq], buf_ref[tm:tm + H, :])
        buf_ref[H:H + tm, :] = p[:, 0:tf]
        gate = _gelu(_conv_from_buf(buf_ref, cw_ref, cb_ref, FFN_CONV, tm))
        h_ref[...] = (gate * p[:, tf:2 * tf]).astype(BF16)
        st_ref[seq] = buf_ref[tm:tm + H, :]

    def sample_tile(par):
        rows = n_sseq * ls
        p = jnp.dot(a_ref[0:rows, :], wb_refs[par][...], preferred_element_type=F32)
        for s in range(n_sseq):
            lo = s * ls
            buf_ref[0:H, :] = hist_ref[n_pseq + s]
            buf_ref[H:H + ls, :] = p[lo:lo + ls, 0:tf]
            gate = _gelu(_conv_from_buf(buf_ref, cw_ref, cb_ref, FFN_CONV, ls))
            h_ref[lo:lo + ls, :] = (gate * p[lo:lo + ls, tf:2 * tf]).astype(BF16)
            st_ref[n_pseq + s] = buf_ref[ls:ls + H, :]

    is_prompt = i < n_ptiles

    @pl.when(jnp.logical_and(q == 0, i == 0))
    def _():
        buf_ref[tm:tm + H, :] = jnp.zeros((H, tf), F32)

    @pl.when(jnp.logical_and(q == 0, is_prompt))
    def _():
        stage(0)

    for par in (0, 1):
        @pl.when(jnp.logical_and(jnp.logical_and(q >= 1, q < nq), jnp.logical_and(q % 2 == par, is_prompt)))
        def _():
            stage(par)
            prompt_tile(1 - par)

        @pl.when(jnp.logical_and(jnp.logical_and(q >= 1, (q - 1) % 2 == par), i == n_ptiles))
        def _():
            sample_tile(par)

    @pl.when(jnp.logical_and(q == nq, is_prompt))
    def _():
        prompt_tile((nq - 1) % 2)


def _ffn_up(hn, w_up, conv_w, conv_b, hist, layer, *, bp, lp, bs, ls):
    m, d = hn.shape
    tf = FFN_COLS
    tm = min(FFN_TILE, lp)
    n_ptiles = bp * lp // tm
    ck = d // n_ptiles
    assert lp % tm == 0 and bs * ls <= tm and ls % BF16_ROWS == 0 and d % n_ptiles == 0 and ck % BF16_ROWS == 0
    nseq = bp + bs
    nq = D_FF // tf
    kern = functools.partial(_ffn_up_kernel, tiles_per_seq=lp // tm, n_ptiles=n_ptiles, n_pseq=bp, n_sseq=bs, ls=ls,
                             tf=tf, nq=nq)
    prev = lambda q: jnp.maximum(q - 1, 0)

    def w_row(q, i):
        return jnp.where(q < nq, jnp.minimum(i, n_ptiles - 1), n_ptiles - 1)

    return pl.pallas_call(
        kern,
        name="ffn_up",
        out_shape=(jax.ShapeDtypeStruct((m, D_FF), BF16), jax.ShapeDtypeStruct((nseq, HIST_ROWS, D_FF), F32)),
        grid=(nq + 1, n_ptiles + 1),
        in_specs=[
            pl.BlockSpec((tm, d), lambda q, i: (jnp.where(q >= 1, i, 0), 0)),
            pl.BlockSpec((None, ck, tf), lambda q, i: (layer, w_row(q, i), jnp.minimum(q, nq - 1))),
            pl.BlockSpec((None, ck, tf), lambda q, i: (layer, w_row(q, i), nq + jnp.minimum(q, nq - 1))),
            pl.BlockSpec((None, FFN_CONV, tf), lambda q, i: (layer, 0, prev(q))),
            pl.BlockSpec((None, 1, tf), lambda q, i: (layer, 0, prev(q))),
            pl.BlockSpec((nseq, HIST_ROWS, tf), lambda q, i: (0, 0, prev(q))),
        ],
        out_specs=(
            pl.BlockSpec((tm, tf), lambda q, i: (jnp.where(q >= 1, i, 0), prev(q))),
            pl.BlockSpec((nseq, HIST_ROWS, tf), lambda q, i: (0, 0, prev(q))),
        ),
        scratch_shapes=[pltpu.VMEM((d, 2 * tf), BF16), pltpu.VMEM((d, 2 * tf), BF16),
                        pltpu.VMEM((tm + HIST_ROWS, tf), F32)],
        compiler_params=pltpu.CompilerParams(dimension_semantics=("arbitrary", "arbitrary"), vmem_limit_bytes=VMEM_LIMIT_BIG),
    )(hn, w_up, w_up, conv_w, conv_b.reshape(conv_b.shape[0], 1, D_FF), hist)


def _lru_kernel(*refs, T, nt, aliased):
    (xa_ref, ga_ref, hist_ref, h0_ref, cw_ref, cb_ref, wr_ref, br_ref, wi_ref, bi_ref, lam_ref) = refs[:11]
    rest = refs[12:] if aliased else refs[11:]
    y_ref, cst_ref, hl_ref, buf_ref, hc_ref = rest
    t = pl.program_id(2)
    tc = xa_ref.shape[1]

    @pl.when(t == 0)
    def _():
        hc_ref[...] = h0_ref[0]

    xc = _conv_step(buf_ref, xa_ref[...], hist_ref, cw_ref, cb_ref, LRU_CONV, t, T)
    xcb = xc.astype(BF16)
    rs, gs = [], []
    for hh in range(tc // LRU_BLOCK):
        xb = xcb[:, hh * LRU_BLOCK:(hh + 1) * LRU_BLOCK]
        rs.append(jnp.dot(xb, wr_ref[hh].astype(BF16), preferred_element_type=F32))
        gs.append(jnp.dot(xb, wi_ref[hh].astype(BF16), preferred_element_type=F32))
    r = jax.nn.sigmoid(jnp.concatenate(rs, axis=-1) + br_ref[...])
    ig = jax.nn.sigmoid(jnp.concatenate(gs, axis=-1) + bi_ref[...])
    log_a = -LRU_C * r * _softplus(-lam_ref[...])
    a = jnp.exp(log_a)
    u = jnp.sqrt(-(jnp.tanh(log_a) * (jnp.exp(2.0 * log_a) + 1.0))) * (ig * xc)

    row = lax.broadcasted_iota(jnp.int32, (T, tc), 0)
    s = 1
    while s < T:
        keep = row >= s
        u = u + a * jnp.where(keep, pltpu.roll(u, s, 0), 0.0)
        a = a * jnp.where(keep, pltpu.roll(a, s, 0), 1.0)
        s *= 2
    h = u + a * hc_ref[...]
    hc_ref[...] = h[T - 1:T, :]
    y_ref[...] = (h * _gelu(ga_ref[...])).astype(BF16)

    @pl.when(t == nt - 1)
    def _():
        cst_ref[0] = buf_ref[T:T + HIST_ROWS, :]
        hl_ref[0] = h[T - 1:T, :]


def _lru_call(proj, y_prev, hist, h0, p, e, *, n_seq, L, T, row0, m):
    nt = L // T
    rb0 = row0 // T
    tc = LRU_TILE
    nc = LRU_WIDTH // tc
    hpt = tc // LRU_BLOCK
    row = lambda s, c, t: rb0 + s * nt + t
    vec = lambda: pl.BlockSpec((None, 1, tc), lambda s, c, t: (e, 0, c))
    in_specs = [
        pl.BlockSpec((T, tc), lambda s, c, t: (row(s, c, t), c)),
        pl.BlockSpec((T, tc), lambda s, c, t: (row(s, c, t), nc + c)),
        pl.BlockSpec((1, HIST_ROWS, tc), lambda s, c, t: (s, 0, c)),
        pl.BlockSpec((1, 1, tc), lambda s, c, t: (s, 0, c)),
        pl.BlockSpec((None, LRU_CONV, tc), lambda s, c, t: (e, 0, c)),
        vec(),
        pl.BlockSpec((None, hpt, LRU_BLOCK, LRU_BLOCK), lambda s, c, t: (e, c, 0, 0)),
        vec(),
        pl.BlockSpec((None, hpt, LRU_BLOCK, LRU_BLOCK), lambda s, c, t: (e, c, 0, 0)),
        vec(),
        vec(),
    ]
    args = [proj, proj, hist, h0, p["lru_conv_w"], p["lru_conv_b"], p["lru_w_r"], p["lru_b_r"], p["lru_w_i"], p["lru_b_i"],
            p["lru_lambda"]]
    aliases = {}
    if y_prev is not None:
        in_specs.append(pl.BlockSpec(memory_space=pl.ANY))
        args.append(y_prev)
        aliases = {len(args) - 1: 0}
    return pl.pallas_call(
        functools.partial(_lru_kernel, T=T, nt=nt, aliased=y_prev is not None),
        name=f"lru_L{L}",
        out_shape=(jax.ShapeDtypeStruct((m, D_MODEL), BF16),
                   jax.ShapeDtypeStruct((n_seq, HIST_ROWS, LRU_WIDTH), F32),
                   jax.ShapeDtypeStruct((n_seq, 1, LRU_WIDTH), F32)),
        grid=(n_seq, nc, nt),
        in_specs=in_specs,
        out_specs=(pl.BlockSpec((T, tc), lambda s, c, t: (row(s, c, t), c)),
                   pl.BlockSpec((1, HIST_ROWS, tc), lambda s, c, t: (s, 0, c)),
                   pl.BlockSpec((1, 1, tc), lambda s, c, t: (s, 0, c))),
        scratch_shapes=[pltpu.VMEM((T + HIST_ROWS, tc), F32), pltpu.VMEM((1, tc), F32)],
        input_output_aliases=aliases,
        compiler_params=pltpu.CompilerParams(dimension_semantics=("arbitrary", "arbitrary", "arbitrary"),
                                             vmem_limit_bytes=VMEM_LIMIT_SMALL),
    )(*args)


def _ssd_kernel(*refs, T, nt):
    (z_ref, x_ref, b_ref, c_ref, dt_ref, hx_ref, hb_ref, hcn_ref, s0_ref,
     cwx_ref, cbx_ref, cwb_ref, cbb_ref, cwc_ref, cbc_ref, dtb_ref, alog_ref, dd_ref, nw_ref, _y_prev,
     y_ref, csx_ref, csb_ref, csc_ref, sf_ref, bufx_ref, bufb_ref, bufc_ref, s_ref, yd_ref) = refs
    t = pl.program_id(2)
    P = SSD_HEAD_DIM
    gw, n = GROUP_WIDTH, SSD_STATE
    width = x_ref.shape[1]

    xs_all = _silu(_conv_step(bufx_ref, x_ref[...], hx_ref, cwx_ref, cbx_ref, SSD_CONV, t, T))
    bm_all = _silu(_conv_step(bufb_ref, b_ref[...], hb_ref, cwb_ref, cbb_ref, SSD_CONV, t, T))
    cm_all = _silu(_conv_step(bufc_ref, c_ref[...], hcn_ref, cwc_ref, cbc_ref, SSD_CONV, t, T))

    @pl.when(t == 0)
    def _():
        s_ref[...] = s0_ref[0]

    dt = _softplus(dt_ref[...] + dtb_ref[...])
    adt = -jnp.exp(alog_ref[...]) * dt
    row = lax.broadcasted_iota(jnp.int32, (T, width), 0)
    acs_all = adt
    s = 1
    while s < T:
        acs_all = acs_all + jnp.where(row >= s, pltpu.roll(acs_all, s, 0), 0.0)
        s *= 2
    xd_all = xs_all * dt
    eacs_all = jnp.exp(acs_all)
    last_all = acs_all[T - 1:T, :]
    xdd_all = xd_all * jnp.exp(last_all - acs_all)
    elast_all = jnp.exp(last_all)
    tri = lax.broadcasted_iota(jnp.int32, (T, T), 0) >= lax.broadcasted_iota(jnp.int32, (T, T), 1)

    for gg in range(SSD_GROUPS_PER_STEP):
        cols = slice(gg * gw, (gg + 1) * gw)
        acs = acs_all[:, cols]
        xd = xd_all[:, cols]
        bmb = bm_all[:, gg * n:(gg + 1) * n].astype(BF16)
        cmb = cm_all[:, gg * n:(gg + 1) * n].astype(BF16)
        cb = lax.dot_general(cmb, bmb, (((1,), (1,)), ((), ())), preferred_element_type=F32)
        for half in range(gw // LANES):
            blk = acs[:, half * LANES:(half + 1) * LANES]
            if T < LANES:
                blk = jnp.concatenate([blk, jnp.zeros((LANES - T, LANES), F32)], axis=0)
            blk_t = blk.T
            for rr in range(LANES // P):
                r = half * (LANES // P) + rr
                col = acs[:, r * P:r * P + 1]
                rowv = blk_t[rr * P:rr * P + 1, :T]
                decay = jnp.exp(jnp.where(tri, col - rowv, -jnp.inf))
                md = (cb * decay).astype(BF16)
                yd_ref[:, gg * gw + r * P:gg * gw + (r + 1) * P] = jnp.dot(
                    md, xd[:, r * P:(r + 1) * P].astype(BF16), preferred_element_type=F32)

        st = s_ref[gg]
        y_off = jnp.dot(cmb, st.astype(BF16), preferred_element_type=F32) * eacs_all[:, cols]
        st_new = st * elast_all[:, cols] + lax.dot_general(
            bmb, xdd_all[:, cols].astype(BF16), (((0,), (0,)), ((), ())), preferred_element_type=F32)
        s_ref[gg] = st_new
        yt = yd_ref[:, cols] + y_off + dd_ref[:, cols] * xs_all[:, cols]
        yt = yt * _silu(z_ref[:, cols])
        yt = yt * lax.rsqrt(jnp.mean(yt * yt, axis=-1, keepdims=True) + NORM_EPS) * nw_ref[:, cols]
        y_ref[:, cols] = yt.astype(BF16)

    @pl.when(t == nt - 1)
    def _():
        csx_ref[0] = bufx_ref[T:T + HIST_ROWS, :]
        csb_ref[0] = bufb_ref[T:T + HIST_ROWS, :]
        csc_ref[0] = bufc_ref[T:T + HIST_ROWS, :]
        sf_ref[0] = s_ref[...]


def _ssd_call(proj, dtr, y_prev, hist, s0, p, e, *, n_seq, L, T, row0, m):
    nt = L // T
    rb0 = row0 // T
    gs = SSD_GROUPS_PER_STEP
    G = SSD_GROUPS
    gw = gs * GROUP_WIDTH
    n = gs * SSD_STATE
    row = lambda s, g, t: rb0 + s * nt + t
    zb = OFF_SSD_Z // gw
    xb = OFF_SSD_XBC // gw
    bb = (OFF_SSD_XBC + SSD_INNER) // n
    cb = (OFF_SSD_XBC + SSD_INNER + SSD_BC) // n
    hb = SSD_INNER // n
    hc = (SSD_INNER + SSD_BC) // n
    vec = lambda: pl.BlockSpec((None, 1, gw), lambda s, g, t: (e, 0, g))
    in_specs = [
        pl.BlockSpec((T, gw), lambda s, g, t: (row(s, g, t), zb + g)),
        pl.BlockSpec((T, gw), lambda s, g, t: (row(s, g, t), xb + g)),
        pl.BlockSpec((T, n), lambda s, g, t: (row(s, g, t), bb + g)),
        pl.BlockSpec((T, n), lambda s, g, t: (row(s, g, t), cb + g)),
        pl.BlockSpec((T, gw), lambda s, g, t: (row(s, g, t), g)),
        pl.BlockSpec((1, HIST_ROWS, gw), lambda s, g, t: (s, 0, g)),
        pl.BlockSpec((1, HIST_ROWS, n), lambda s, g, t: (s, 0, hb + g)),
        pl.BlockSpec((1, HIST_ROWS, n), lambda s, g, t: (s, 0, hc + g)),
        pl.BlockSpec((1, gs, SSD_STATE, GROUP_WIDTH), lambda s, g, t: (s, g, 0, 0)),
        pl.BlockSpec((None, SSD_CONV, gw), lambda s, g, t: (e, 0, g)),
        pl.BlockSpec((None, 1, gw), lambda s, g, t: (e, 0, g)),
        pl.BlockSpec((None, SSD_CONV, n), lambda s, g, t: (e, 0, hb + g)),
        pl.BlockSpec((None, 1, n), lambda s, g, t: (e, 0, hb + g)),
        pl.BlockSpec((None, SSD_CONV, n), lambda s, g, t: (e, 0, hc + g)),
        pl.BlockSpec((None, 1, n), lambda s, g, t: (e, 0, hc + g)),
        vec(), vec(), vec(), vec(),
        pl.BlockSpec(memory_space=pl.ANY),
    ]
    args = [proj, proj, proj, proj, dtr, hist, hist, hist, s0,
            p["ssd_conv_w"], p["ssd_conv_b"], p["ssd_conv_w"], p["ssd_conv_b"], p["ssd_conv_w"], p["ssd_conv_b"],
            p["ssd_dt_bias_x"], p["ssd_a_log_x"], p["ssd_d_x"], p["ssd_norm_w"], y_prev]
    yb = LRU_WIDTH // gw
    return pl.pallas_call(
        functools.partial(_ssd_kernel, T=T, nt=nt),
        name=f"ssd_L{L}",
        out_shape=(jax.ShapeDtypeStruct((m, D_MODEL), BF16),
                   jax.ShapeDtypeStruct((n_seq, HIST_ROWS, SSD_INNER), F32),
                   jax.ShapeDtypeStruct((n_seq, HIST_ROWS, SSD_BC), F32),
                   jax.ShapeDtypeStruct((n_seq, HIST_ROWS, SSD_BC), F32),
                   jax.ShapeDtypeStruct((n_seq, G, SSD_STATE, GROUP_WIDTH), F32)),
        grid=(n_seq, G // gs, nt),
        in_specs=in_specs,
        out_specs=(pl.BlockSpec((T, gw), lambda s, g, t: (row(s, g, t), yb + g)),
                   pl.BlockSpec((1, HIST_ROWS, gw), lambda s, g, t: (s, 0, g)),
                   pl.BlockSpec((1, HIST_ROWS, n), lambda s, g, t: (s, 0, g)),
                   pl.BlockSpec((1, HIST_ROWS, n), lambda s, g, t: (s, 0, g)),
                   pl.BlockSpec((1, gs, SSD_STATE, GROUP_WIDTH), lambda s, g, t: (s, g, 0, 0))),
        scratch_shapes=[pltpu.VMEM((T + HIST_ROWS, gw), F32), pltpu.VMEM((T + HIST_ROWS, n), F32),
                        pltpu.VMEM((T + HIST_ROWS, n), F32), pltpu.VMEM((gs, SSD_STATE, GROUP_WIDTH), F32),
                        pltpu.VMEM((T, gw), F32)],
        input_output_aliases={len(args) - 1: 0},
        compiler_params=pltpu.CompilerParams(dimension_semantics=("arbitrary", "arbitrary", "arbitrary"),
                                             vmem_limit_bytes=VMEM_LIMIT_SMALL),
    )(*args)


def _ret_kernel(*refs, nt, n_aliased):
    (q_ref, k_ref, v_ref, g_ref, cos_ref, sin_ref, dm_ref, cr_ref, kd_ref, cd_ref, s0_ref) = refs[:11]
    y_ref, sf_ref, s_ref = refs[11 + n_aliased:]
    t = pl.program_id(2)
    half = RET_QK_DIM // 2
    dk, dv = RET_QK_DIM, RET_V_DIM

    @pl.when(t == 0)
    def _():
        s_ref[...] = s0_ref[0]

    cos = cos_ref[...]
    sin = sin_ref[...]

    def rot(x):
        x1 = x[:, :half]
        x2 = x[:, half:]
        return jnp.concatenate([x1 * cos - x2 * sin, x2 * cos + x1 * sin], axis=-1)

    for hh in range(RET_HEADS_PER_STEP):
        q = rot(q_ref[:, hh * dk:(hh + 1) * dk])
        k = rot(k_ref[:, hh * dk:(hh + 1) * dk]) * (RET_QK_DIM ** -0.5)
        qb = q.astype(BF16)
        vb = v_ref[:, hh * dv:(hh + 1) * dv].astype(BF16)
        att = lax.dot_general(qb, k.astype(BF16), (((1,), (1,)), ((), ())), preferred_element_type=F32) * dm_ref[hh]
        st = s_ref[hh]
        o = (jnp.dot(att.astype(BF16), vb, preferred_element_type=F32)
             + jnp.dot((q * cr_ref[hh]).astype(BF16), st.astype(BF16), preferred_element_type=F32))
        st_new = st * cd_ref[hh][0:1, :] + lax.dot_general((k * kd_ref[hh]).astype(BF16), vb, (((0,), (0,)), ((), ())),
                                                            preferred_element_type=F32)
        s_ref[hh] = st_new
        o = o * lax.rsqrt(jnp.mean(o * o, axis=-1, keepdims=True) + NORM_EPS)
        y_ref[:, hh * dv:(hh + 1) * dv] = (_silu(g_ref[:, hh * dv:(hh + 1) * dv]) * o).astype(BF16)

    @pl.when(t == nt - 1)
    def _():
        sf_ref[0] = s_ref[...]


def _ret_tables(T, ct, pos0, L):
    f32 = F32
    log_gamma = jnp.log1p(-jnp.exp2(-5.0 - jnp.arange(RET_HEADS, dtype=f32)))
    n = jnp.arange(T, dtype=f32)
    chunk = jnp.arange(T) // ct
    same = chunk[:, None] == chunk[None, :]
    earlier = chunk[None, :] < chunk[:, None]
    diff = n[:, None] - n[None, :]
    lg = log_gamma[:, None, None]
    dm = jnp.where(same[None], jnp.exp(jnp.abs(diff)[None] * lg), jnp.where(earlier[None], jnp.exp(diff[None] * lg), 0.0))
    cross = jnp.exp((n[None, :] + 1.0) * log_gamma[:, None])
    kvd = jnp.exp((T - 1.0 - n)[None, :] * log_gamma[:, None])
    cdec = jnp.exp(T * log_gamma)
    cr = jnp.broadcast_to(cross[:, :, None], (RET_HEADS, T, RET_QK_DIM))
    kd = jnp.broadcast_to(kvd[:, :, None], (RET_HEADS, T, RET_QK_DIM))
    cd = jnp.broadcast_to(cdec[:, None, None], (RET_HEADS, SUBLANES, RET_V_DIM))
    half = RET_QK_DIM // 2
    inv = 1.0 / (ROPE_BASE ** jnp.linspace(0.0, 1.0, half, dtype=f32))
    pos = float(pos0) + jnp.arange(L, dtype=f32)
    ang = pos[:, None] * inv[None, :]
    return dm, cr, kd, cd, jnp.cos(ang), jnp.sin(ang)


def _ret_call(proj, y_prev, s0, e0, sf_prev, e, n_layers, *, n_seq, L, T, row0, pos0, m):
    nt = L // T
    rb0 = row0 // T
    ct = min(CHUNK, L)
    assert T % ct == 0
    dm, cr, kd, cd, cos, sin = _ret_tables(T, ct, pos0, L)
    H = RET_HEADS
    hs = RET_HEADS_PER_STEP
    dk, dv = hs * RET_QK_DIM, hs * RET_V_DIM
    row = lambda s, h, t: rb0 + s * nt + t
    kb0 = RET_QK_WIDTH // dk
    vb0 = 2 * RET_QK_WIDTH // dv
    gb0 = (2 * RET_QK_WIDTH + RET_V_WIDTH) // dv
    in_specs = [
        pl.BlockSpec((T, dk), lambda s, h, t: (row(s, h, t), h)),
        pl.BlockSpec((T, dk), lambda s, h, t: (row(s, h, t), kb0 + h)),
        pl.BlockSpec((T, dv), lambda s, h, t: (row(s, h, t), vb0 + h)),
        pl.BlockSpec((T, dv), lambda s, h, t: (row(s, h, t), gb0 + h)),
        pl.BlockSpec((T, RET_QK_DIM // 2), lambda s, h, t: (t, 0)),
        pl.BlockSpec((T, RET_QK_DIM // 2), lambda s, h, t: (t, 0)),
        pl.BlockSpec((hs, T, T), lambda s, h, t: (h, 0, 0)),
        pl.BlockSpec((hs, T, RET_QK_DIM), lambda s, h, t: (h, 0, 0)),
        pl.BlockSpec((hs, T, RET_QK_DIM), lambda s, h, t: (h, 0, 0)),
        pl.BlockSpec((hs, SUBLANES, RET_V_DIM), lambda s, h, t: (h, 0, 0)),
        pl.BlockSpec((None, 1, hs, RET_QK_DIM, RET_V_DIM), lambda s, h, t: (e0, s, h, 0, 0)),
    ]
    args = [proj, proj, proj, proj, cos, sin, dm, cr, kd, cd, s0]
    aliases = {}
    for out_idx, prev in ((0, y_prev), (1, sf_prev)):
        if prev is not None:
            in_specs.append(pl.BlockSpec(memory_space=pl.ANY))
            args.append(prev)
            aliases[len(args) - 1] = out_idx
    return pl.pallas_call(
        functools.partial(_ret_kernel, nt=nt, n_aliased=len(aliases)),
        name=f"ret_L{L}",
        out_shape=(jax.ShapeDtypeStruct((m, RET_V_WIDTH), BF16),
                   jax.ShapeDtypeStruct((n_layers, n_seq, H, RET_QK_DIM, RET_V_DIM), F32)),
        grid=(n_seq, H // hs, nt),
        in_specs=in_specs,
        out_specs=(pl.BlockSpec((T, dv), lambda s, h, t: (row(s, h, t), h)),
                   pl.BlockSpec((None, 1, hs, RET_QK_DIM, RET_V_DIM), lambda s, h, t: (e, s, h, 0, 0))),
        scratch_shapes=[pltpu.VMEM((hs, RET_QK_DIM, RET_V_DIM), F32)],
        input_output_aliases=aliases,
        compiler_params=pltpu.CompilerParams(dimension_semantics=("arbitrary", "arbitrary", "arbitrary"),
                                             vmem_limit_bytes=VMEM_LIMIT_SMALL),
    )(*args)


def _pad_hist(state, n_zero_seq):
    n, km1, c = state.shape
    padded = jnp.pad(state, ((n_zero_seq, 0), (HIST_ROWS - km1, 0), (0, 0)))
    return padded


def _ssd_state_in(s):
    n = s.shape[0]
    s = s.reshape(n, SSD_GROUPS, HEADS_PER_GROUP, SSD_HEAD_DIM, SSD_STATE)
    return s.transpose(0, 1, 4, 2, 3).reshape(n, SSD_GROUPS, SSD_STATE, GROUP_WIDTH)


def _ssd_state_out(s):
    n = s.shape[0]
    s = s.reshape(n, SSD_GROUPS, SSD_STATE, HEADS_PER_GROUP, SSD_HEAD_DIM)
    return s.transpose(0, 1, 3, 4, 2).reshape(n, SSD_HEADS, SSD_HEAD_DIM, SSD_STATE)


def kernel(x_prompt, x_sample, state_lru_conv, state_lru_h, state_ssd_conv, state_ssd, state_ret, state_ffn_conv,
           norm_mix_pre, norm_mix_post, norm_ffn_pre, norm_ffn_post, w_in_ab, lru_conv_w, lru_conv_b, lru_w_r, lru_b_r,
           lru_w_i, lru_b_i, lru_lambda, ssd_conv_w, ssd_conv_b, ssd_dt_bias, ssd_a_log, ssd_d, ssd_norm_w, w_out_ab,
           w_in_ret, w_out_ret, ffn_w_up, ffn_conv_w, ffn_conv_b, ffn_w_down):
    bp, lp, d = x_prompt.shape
    bs, ls, _ = x_sample.shape
    depth = norm_mix_pre.shape[0]
    n_even = w_in_ab.shape[0]
    mp = bp * lp
    m = mp + bs * ls

    groups = ((bp, lp, 0, 0), (bs, ls, mp, PAST_LEN))
    t_lru = (min(256, lp), ls)
    t_ssd = (min(128, lp), ls)
    t_ret = (min(256, lp), ls)

    rep = lambda v: jnp.repeat(v, SSD_HEAD_DIM, axis=-1).reshape(n_even, 1, SSD_INNER)
    vec3 = lambda v: v.reshape(v.shape[0], 1, v.shape[-1])
    ab = {
        "lru_conv_w": lru_conv_w, "lru_conv_b": vec3(lru_conv_b), "lru_w_r": lru_w_r, "lru_b_r": vec3(lru_b_r),
        "lru_w_i": lru_w_i, "lru_b_i": vec3(lru_b_i), "lru_lambda": vec3(lru_lambda),
        "ssd_conv_w": ssd_conv_w, "ssd_conv_b": vec3(ssd_conv_b),
        "ssd_dt_bias_x": rep(ssd_dt_bias), "ssd_a_log_x": rep(ssd_a_log), "ssd_d_x": rep(ssd_d),
        "ssd_norm_w": vec3(ssd_norm_w),
    }
    ffn_hist = _pad_hist(state_ffn_conv.reshape(depth * bs, FFN_CONV - 1, D_FF), 0).reshape(depth, bs, HIST_ROWS, D_FF)
    ffn_hist = jnp.pad(ffn_hist, ((0, 0), (bp, 0), (0, 0), (0, 0)))

    new = {k: ([], []) for k in ("lc", "lh", "sc", "ss", "fc")}
    ret_stack = [None, None]
    w_in_ab_t = jnp.swapaxes(w_in_ab, 1, 2)

    x, hn = _first_norm(x_prompt.reshape(mp, d), x_sample.reshape(bs * ls, d), norm_mix_pre[0])
    for layer in range(depth):
        e = layer // 2
        if layer % 2 == 0:
            proj = _matmul(hn, w_in_ab_t, e, n_cols=OFF_SSD_DT, w_transposed=True)
            dtr = _dt_proj(hn, w_in_ab_t, e)
            y = None
            for gi, (n_seq, L, row0, _) in enumerate(groups):
                if gi == 0:
                    hist = jnp.zeros((n_seq, HIST_ROWS, LRU_WIDTH), F32)
                    h0 = jnp.zeros((n_seq, 1, LRU_WIDTH), F32)
                else:
                    hist = _pad_hist(state_lru_conv[e], 0)
                    h0 = state_lru_h[e].reshape(n_seq, 1, LRU_WIDTH)
                y, cst, hl = _lru_call(proj, y, hist, h0, ab, e, n_seq=n_seq, L=L, T=t_lru[gi], row0=row0, m=m)
                new["lc"][gi].append(cst[:, HIST_ROWS - (LRU_CONV - 1):, :])
                new["lh"][gi].append(hl.reshape(n_seq, LRU_WIDTH))
            for gi, (n_seq, L, row0, _) in enumerate(groups):
                if gi == 0:
                    hist = jnp.zeros((n_seq, HIST_ROWS, SSD_CONV_DIM), F32)
                    s0 = jnp.zeros((n_seq, SSD_GROUPS, SSD_STATE, GROUP_WIDTH), F32)
                else:
                    hist = _pad_hist(state_ssd_conv[e], 0)
                    s0 = _ssd_state_in(state_ssd[e])
                y, csx, csb, csc, sf = _ssd_call(proj, dtr, y, hist, s0, ab, e, n_seq=n_seq, L=L, T=t_ssd[gi], row0=row0, m=m)
                cst = jnp.concatenate([csx, csb, csc], axis=-1)
                new["sc"][gi].append(cst[:, HIST_ROWS - (SSD_CONV - 1):, :])
                new["ss"][gi].append(_ssd_state_out(sf))
            mix = _matmul(y, w_out_ab, e)
        else:
            proj = _matmul(hn, w_in_ret, e)
            y = None
            for gi, (n_seq, L, row0, pos0) in enumerate(groups):
                if gi == 0:
                    s0, e0 = jnp.zeros((1, n_seq, RET_HEADS, RET_QK_DIM, RET_V_DIM), F32), 0
                else:
                    s0, e0 = state_ret, e
                y, ret_stack[gi] = _ret_call(proj, y, s0, e0, ret_stack[gi], e, depth // 2, n_seq=n_seq, L=L, T=t_ret[gi],
                                             row0=row0, pos0=pos0, m=m)
            mix = _matmul(y, w_out_ret, e)
        x, hn = _row_call(_resid_norm_kernel, [x, mix], [norm_mix_post[layer], norm_ffn_pre[layer]], [F32, BF16])
        h, fst = _ffn_up(hn, ffn_w_up, ffn_conv_w, ffn_conv_b, ffn_hist[layer], layer, bp=bp, lp=lp, bs=bs, ls=ls)
        fst = fst[:, HIST_ROWS - (FFN_CONV - 1):, :]
        new["fc"][0].append(fst[:bp])
        new["fc"][1].append(fst[bp:])
        f = _matmul(h, ffn_w_down, layer)
        if layer + 1 < depth:
            x, hn = _row_call(_resid_norm_kernel, [x, f], [norm_ffn_post[layer], norm_mix_pre[layer + 1]], [F32, BF16])
        else:
            y_prompt, y_sample = _last_resid(x, f, norm_ffn_post[layer], mp)

    outs = [y_prompt.reshape(bp, lp, d), y_sample.reshape(bs, ls, d)]
    for gi in range(2):
        for k in ("lc", "lh", "sc", "ss", "rs", "fc"):
            outs.append(ret_stack[gi] if k == "rs" else jnp.stack(new[k][gi]))
    return tuple(outs)
```

```python
import functools

import jax
import jax.numpy as jnp
from jax import lax
from jax.experimental import pallas as pl
from jax.experimental.pallas import tpu as pltpu

F32 = jnp.float32
BF16 = jnp.bfloat16

D_MODEL = 4096
PAST_LEN = 1024
CHUNK = 64
NORM_EPS = 1e-6
LRU_WIDTH = D_MODEL // 2
LRU_HEADS = 16
LRU_BLOCK = LRU_WIDTH // LRU_HEADS
LRU_CONV = 4
LRU_C = 8.0
SSD_INNER = D_MODEL // 2
SSD_HEAD_DIM = 64
SSD_HEADS = SSD_INNER // SSD_HEAD_DIM
SSD_GROUPS = 8
SSD_STATE = 128
SSD_CONV = 4
SSD_BC = SSD_GROUPS * SSD_STATE
SSD_CONV_DIM = SSD_INNER + 2 * SSD_BC
OFF_SSD_Z = 2 * LRU_WIDTH
OFF_SSD_XBC = OFF_SSD_Z + SSD_INNER
OFF_SSD_DT = OFF_SSD_XBC + SSD_CONV_DIM
RET_HEADS = 16
RET_QK_DIM = D_MODEL // RET_HEADS
RET_V_DIM = 2 * RET_QK_DIM
RET_QK_WIDTH = RET_HEADS * RET_QK_DIM
RET_V_WIDTH = RET_HEADS * RET_V_DIM
ROPE_BASE = 10000.0
D_FF = 3 * D_MODEL
FFN_CONV = 3

SUBLANES = 8
LANES = 128
BF16_ROWS = 16
HIST_ROWS = SUBLANES
MIB = 1 << 20
VMEM_LIMIT_BIG = 56 * MIB
VMEM_LIMIT_SMALL = 40 * MIB

HEADS_PER_GROUP = SSD_HEADS // SSD_GROUPS
GROUP_WIDTH = HEADS_PER_GROUP * SSD_HEAD_DIM
LRU_TILE = 512
MATMUL_TM_CAP = 1056
MATMUL_TK = 4096
MATMUL_TN = 1024
FFN_TILE = 1024
FFN_COLS = 512
RET_HEADS_PER_STEP = 4
SSD_GROUPS_PER_STEP = 2


def _pick_tile(total, cap, mult):
    best = None
    for t in range(mult, min(total, cap) + 1, mult):
        if total % t == 0:
            best = t
    assert best is not None, (total, cap, mult)
    return best


def _silu(x):
    return x * jax.nn.sigmoid(x)


def _gelu(x):
    return jax.nn.gelu(x, approximate=True)


def _softplus(x):
    return jnp.maximum(x, 0.0) + jnp.log1p(jnp.exp(-jnp.abs(x)))


def _rms(x, w):
    return x * lax.rsqrt(jnp.mean(x * x, axis=-1, keepdims=True) + NORM_EPS) * w


def _resid_norm_kernel(x_ref, m_ref, wp_ref, wn_ref, xo_ref, hn_ref):
    xn = x_ref[...] + _rms(m_ref[...].astype(F32), wp_ref[...])
    xo_ref[...] = xn
    hn_ref[...] = _rms(xn, wn_ref[...]).astype(BF16)


def _first_norm_kernel(xp_ref, xs_ref, w_ref, xo_ref, hn_ref, *, n_ptiles):
    i = pl.program_id(0)

    def emit(x):
        xo_ref[...] = x
        hn_ref[...] = _rms(x, w_ref[...]).astype(BF16)

    @pl.when(i < n_ptiles)
    def _():
        emit(xp_ref[...])

    @pl.when(i >= n_ptiles)
    def _():
        emit(xs_ref[...])


def _last_resid_kernel(x_ref, m_ref, wp_ref, yp_ref, ys_ref, *, n_ptiles):
    i = pl.program_id(0)
    xn = x_ref[...] + _rms(m_ref[...].astype(F32), wp_ref[...])

    @pl.when(i < n_ptiles)
    def _():
        yp_ref[...] = xn

    @pl.when(i >= n_ptiles)
    def _():
        ys_ref[...] = xn


def _row_tile(mp, ms):
    tr = _pick_tile(mp, 256, BF16_ROWS)
    assert ms % tr == 0
    return tr


def _first_norm(xp, xs, w):
    mp, d = xp.shape
    ms = xs.shape[0]
    tr = _row_tile(mp, ms)
    n_ptiles = mp // tr
    row_spec = pl.BlockSpec((tr, d), lambda i: (i, 0))
    return pl.pallas_call(
        functools.partial(_first_norm_kernel, n_ptiles=n_ptiles),
        name="first_norm",
        out_shape=(jax.ShapeDtypeStruct((mp + ms, d), F32), jax.ShapeDtypeStruct((mp + ms, d), BF16)),
        grid=((mp + ms) // tr,),
        in_specs=[pl.BlockSpec((tr, d), lambda i: (jnp.minimum(i, n_ptiles - 1), 0)),
                  pl.BlockSpec((tr, d), lambda i: (jnp.maximum(i - n_ptiles, 0), 0)),
                  pl.BlockSpec((1, d), lambda i: (0, 0))],
        out_specs=(row_spec, row_spec),
        compiler_params=pltpu.CompilerParams(dimension_semantics=("arbitrary",), vmem_limit_bytes=VMEM_LIMIT_SMALL),
    )(xp, xs, w.reshape(1, d))


def _last_resid(x, mix, w, mp):
    m, d = x.shape
    ms = m - mp
    tr = _row_tile(mp, ms)
    n_ptiles = mp // tr
    row_spec = pl.BlockSpec((tr, d), lambda i: (i, 0))
    return pl.pallas_call(
        functools.partial(_last_resid_kernel, n_ptiles=n_ptiles),
        name="last_resid",
        out_shape=(jax.ShapeDtypeStruct((mp, d), F32), jax.ShapeDtypeStruct((ms, d), F32)),
        grid=(m // tr,),
        in_specs=[row_spec, row_spec, pl.BlockSpec((1, d), lambda i: (0, 0))],
        out_specs=(pl.BlockSpec((tr, d), lambda i: (jnp.minimum(i, n_ptiles - 1), 0)),
                   pl.BlockSpec((tr, d), lambda i: (jnp.maximum(i - n_ptiles, 0), 0))),
        compiler_params=pltpu.CompilerParams(dimension_semantics=("arbitrary",), vmem_limit_bytes=VMEM_LIMIT_SMALL),
    )(x, mix, w.reshape(1, d))


def _row_call(kernel_fn, arrays, weights, out_dtypes):
    m, d = arrays[0].shape
    tr = _pick_tile(m, 256, BF16_ROWS)
    row_spec = pl.BlockSpec((tr, d), lambda i: (i, 0))
    w_spec = pl.BlockSpec((1, d), lambda i: (0, 0))
    outs = tuple(jax.ShapeDtypeStruct((m, d), dt) for dt in out_dtypes)
    res = pl.pallas_call(
        kernel_fn,
        name=kernel_fn.__name__.strip("_"),
        out_shape=outs,
        grid=(m // tr,),
        in_specs=[row_spec] * len(arrays) + [w_spec] * len(weights),
        out_specs=tuple(row_spec for _ in outs),
        compiler_params=pltpu.CompilerParams(dimension_semantics=("arbitrary",), vmem_limit_bytes=VMEM_LIMIT_SMALL),
    )(*arrays, *[w.reshape(1, d) for w in weights])
    return res


def _matmul_kernel(a_ref, w_ref, o_ref, wb0_ref, wb1_ref, *, nq, ck, w_transposed):
    q = pl.program_id(0)
    i = pl.program_id(1)
    wb_refs = (wb0_ref, wb1_ref)

    def stage(par):
        w = w_ref[...]
        if w_transposed:
            w = w.T
        wb_refs[par][pl.ds(pl.multiple_of(i * ck, BF16_ROWS), ck), :] = w.astype(BF16)

    def product(par):
        o_ref[...] = jnp.dot(a_ref[...], wb_refs[par][...], preferred_element_type=F32).astype(o_ref.dtype)

    @pl.when(q == 0)
    def _():
        stage(0)

    for par in (0, 1):
        @pl.when(jnp.logical_and(jnp.logical_and(q >= 1, q < nq), q % 2 == par))
        def _():
            stage(par)
            product(1 - par)

    @pl.when(q == nq)
    def _():
        product((nq - 1) % 2)


def _matmul(a, w, layer, *, n_cols=None, out_dtype=F32, w_transposed=False, tag=""):
    m, k = a.shape
    n = n_cols if n_cols is not None else w.shape[-2 if w_transposed else -1]
    depth_ratio = max(1, k // MATMUL_TK)
    tm = _pick_tile(m, MATMUL_TM_CAP // depth_ratio, BF16_ROWS)
    tn = MATMUL_TN if depth_ratio == 1 else MATMUL_TN // 2
    n_mt = m // tm
    ck = k // n_mt
    assert n % tn == 0 and k % n_mt == 0 and ck % LANES == 0
    nq = n // tn

    def a_map(q, i):
        return (jnp.where(q >= 1, i, 0), 0)

    def w_map(q, i):
        row = jnp.where(q < nq, i, n_mt - 1)
        p = jnp.minimum(q, nq - 1)
        return (layer, p, row) if w_transposed else (layer, row, p)

    def o_map(q, i):
        return (jnp.where(q >= 1, i, 0), jnp.maximum(q - 1, 0))

    return pl.pallas_call(
        functools.partial(_matmul_kernel, nq=nq, ck=ck, w_transposed=w_transposed),
        name=f"matmul_k{k}_n{n}{tag}",
        out_shape=jax.ShapeDtypeStruct((m, n), out_dtype),
        grid=(nq + 1, n_mt),
        in_specs=[
            pl.BlockSpec((tm, k), a_map),
            pl.BlockSpec((None, tn, ck) if w_transposed else (None, ck, tn), w_map),
        ],
        out_specs=pl.BlockSpec((tm, tn), o_map),
        scratch_shapes=[pltpu.VMEM((k, tn), BF16), pltpu.VMEM((k, tn), BF16)],
        compiler_params=pltpu.CompilerParams(
            dimension_semantics=("arbitrary", "arbitrary"), vmem_limit_bytes=VMEM_LIMIT_BIG),
    )(a, w)


def _dt_kernel(hn_ref, w_ref, b_ref, o_ref):
    tm = hn_ref.shape[0]
    dt = lax.dot_general(hn_ref[...], w_ref[...].astype(BF16), (((1,), (1,)), ((), ())),
                         preferred_element_type=F32)
    dt = _softplus(dt + b_ref[...])
    for h in range(SSD_HEADS):
        o_ref[:, h * SSD_HEAD_DIM:(h + 1) * SSD_HEAD_DIM] = jnp.broadcast_to(dt[:, h:h + 1], (tm, SSD_HEAD_DIM))


def _dt_proj(hn, w_t, dt_bias, layer):
    m, d = hn.shape
    tm = _pick_tile(m, MATMUL_TM_CAP, BF16_ROWS)
    assert OFF_SSD_DT % SSD_HEADS == 0
    return pl.pallas_call(
        _dt_kernel,
        name="dt_proj",
        out_shape=jax.ShapeDtypeStruct((m, SSD_INNER), F32),
        grid=(m // tm,),
        in_specs=[pl.BlockSpec((tm, d), lambda i: (i, 0)),
                  pl.BlockSpec((None, SSD_HEADS, d), lambda i: (layer, OFF_SSD_DT // SSD_HEADS, 0)),
                  pl.BlockSpec((None, 1, SSD_HEADS), lambda i: (layer, 0, 0))],
        out_specs=pl.BlockSpec((tm, SSD_INNER), lambda i: (i, 0)),
        compiler_params=pltpu.CompilerParams(dimension_semantics=("arbitrary",), vmem_limit_bytes=VMEM_LIMIT_SMALL),
    )(hn, w_t, dt_bias.reshape(dt_bias.shape[0], 1, SSD_HEADS))


def _conv_from_buf(buf_ref, cw_ref, cb_ref, width, rows):
    base = HIST_ROWS - (width - 1)
    y = cb_ref[...] + buf_ref[base:base + rows, :] * cw_ref[0:1, :]
    for k in range(1, width):
        y = y + buf_ref[base + k:base + k + rows, :] * cw_ref[k:k + 1, :]
    return y


def _conv_step(buf_ref, src, hist_ref, cw_ref, cb_ref, width, t, rows):
    @pl.when(t == 0)
    def _():
        buf_ref[0:HIST_ROWS, :] = hist_ref[0]

    @pl.when(t > 0)
    def _():
        buf_ref[0:HIST_ROWS, :] = buf_ref[rows:rows + HIST_ROWS, :]

    buf_ref[HIST_ROWS:HIST_ROWS + rows, :] = src
    return _conv_from_buf(buf_ref, cw_ref, cb_ref, width, rows)


def _ffn_up_kernel(a_ref, wg_ref, wu_ref, cw_ref, cb_ref, hist_ref, h_ref, st_ref, wb0_ref, wb1_ref, buf_ref, *,
                   tiles_per_seq, n_ptiles, n_pseq, n_sseq, ls, tf, nq):
    q = pl.program_id(0)
    i = pl.program_id(1)
    tm = a_ref.shape[0]
    ck = wg_ref.shape[0]
    wb_refs = (wb0_ref, wb1_ref)
    H = HIST_ROWS

    def stage(par):
        rows = pl.ds(pl.multiple_of(i * ck, BF16_ROWS), ck)
        wb_refs[par][rows, 0:tf] = wg_ref[...].astype(BF16)
        wb_refs[par][rows, tf:2 * tf] = wu_ref[...].astype(BF16)

    def prompt_tile(par):
        p = jnp.dot(a_ref[...], wb_refs[par][...], preferred_element_type=F32)
        seq = i // tiles_per_seq
        first = (i % tiles_per_seq) == 0
        buf_ref[0:H, :] = jnp.where(first, hist_ref[seq], buf_ref[tm:tm + H, :])
        buf_ref[H:H + tm, :] = p[:, 0:tf]
        gate = _gelu(_conv_from_buf(buf_ref, cw_ref, cb_ref, FFN_CONV, tm))
        h_ref[...] = (gate * p[:, tf:2 * tf]).astype(BF16)
        st_ref[seq] = buf_ref[tm:tm + H, :]

    def sample_tile(par):
        rows = n_sseq * ls
        p = jnp.dot(a_ref[0:rows, :], wb_refs[par][...], preferred_element_type=F32)
        for s in range(n_sseq):
            lo = s * ls
            buf_ref[0:H, :] = hist_ref[n_pseq + s]
            buf_ref[H:H + ls, :] = p[lo:lo + ls, 0:tf]
            gate = _gelu(_conv_from_buf(buf_ref, cw_ref, cb_ref, FFN_CONV, ls))
            h_ref[lo:lo + ls, :] = (gate * p[lo:lo + ls, tf:2 * tf]).astype(BF16)
            st_ref[n_pseq + s] = buf_ref[ls:ls + H, :]

    is_prompt = i < n_ptiles

    @pl.when(jnp.logical_and(q == 0, i == 0))
    def _():
        buf_ref[tm:tm + H, :] = jnp.zeros((H, tf), F32)

    @pl.when(jnp.logical_and(q == 0, is_prompt))
    def _():
        stage(0)

    for par in (0, 1):
        @pl.when(jnp.logical_and(jnp.logical_and(q >= 1, q < nq), jnp.logical_and(q % 2 == par, is_prompt)))
        def _():
            stage(par)
            prompt_tile(1 - par)

        @pl.when(jnp.logical_and(jnp.logical_and(q >= 1, (q - 1) % 2 == par), i == n_ptiles))
        def _():
            sample_tile(par)

    @pl.when(jnp.logical_and(q == nq, is_prompt))
    def _():
        prompt_tile((nq - 1) % 2)


def _ffn_up(hn, w_up, conv_w, conv_b, hist, layer, *, bp, lp, bs, ls):
    m, d = hn.shape
    tf = FFN_COLS
    tm = min(FFN_TILE, lp)
    n_ptiles = bp * lp // tm
    ck = d // n_ptiles
    assert lp % tm == 0 and bs * ls <= tm and ls % BF16_ROWS == 0 and d % n_ptiles == 0 and ck % BF16_ROWS == 0
    nseq = bp + bs
    nq = D_FF // tf
    kern = functools.partial(_ffn_up_kernel, tiles_per_seq=lp // tm, n_ptiles=n_ptiles, n_pseq=bp, n_sseq=bs, ls=ls,
                             tf=tf, nq=nq)
    prev = lambda q: jnp.maximum(q - 1, 0)

    def w_row(q, i):
        return jnp.where(q < nq, jnp.minimum(i, n_ptiles - 1), n_ptiles - 1)

    return pl.pallas_call(
        kern,
        name="ffn_up",
        out_shape=(jax.ShapeDtypeStruct((m, D_FF), BF16), jax.ShapeDtypeStruct((nseq, HIST_ROWS, D_FF), F32)),
        grid=(nq + 1, n_ptiles + 1),
        in_specs=[
            pl.BlockSpec((tm, d), lambda q, i: (jnp.where(q >= 1, i, 0), 0)),
            pl.BlockSpec((None, ck, tf), lambda q, i: (layer, w_row(q, i), jnp.minimum(q, nq - 1))),
            pl.BlockSpec((None, ck, tf), lambda q, i: (layer, w_row(q, i), nq + jnp.minimum(q, nq - 1))),
            pl.BlockSpec((None, FFN_CONV, tf), lambda q, i: (layer, 0, prev(q))),
            pl.BlockSpec((None, 1, tf), lambda q, i: (layer, 0, prev(q))),
            pl.BlockSpec((nseq, HIST_ROWS, tf), lambda q, i: (0, 0, prev(q))),
        ],
        out_specs=(
            pl.BlockSpec((tm, tf), lambda q, i: (jnp.where(q >= 1, i, 0), prev(q))),
            pl.BlockSpec((nseq, HIST_ROWS, tf), lambda q, i: (0, 0, prev(q))),
        ),
        scratch_shapes=[pltpu.VMEM((d, 2 * tf), BF16), pltpu.VMEM((d, 2 * tf), BF16),
                        pltpu.VMEM((tm + HIST_ROWS, tf), F32)],
        compiler_params=pltpu.CompilerParams(dimension_semantics=("arbitrary", "arbitrary"), vmem_limit_bytes=VMEM_LIMIT_BIG),
    )(hn, w_up, w_up, conv_w, conv_b.reshape(conv_b.shape[0], 1, D_FF), hist)


def _lru_kernel(*refs, T, nt, aliased):
    (xa_ref, ga_ref, hist_ref, h0_ref, cw_ref, cb_ref, wr_ref, br_ref, wi_ref, bi_ref, lam_ref) = refs[:11]
    rest = refs[12:] if aliased else refs[11:]
    y_ref, cst_ref, hl_ref, buf_ref, hc_ref = rest
    t = pl.program_id(2)
    tc = xa_ref.shape[1]

    @pl.when(t == 0)
    def _():
        hc_ref[...] = h0_ref[0]

    xc = _conv_step(buf_ref, xa_ref[...], hist_ref, cw_ref, cb_ref, LRU_CONV, t, T)
    xcb = xc.astype(BF16)
    rs, gs = [], []
    for hh in range(tc // LRU_BLOCK):
        xb = xcb[:, hh * LRU_BLOCK:(hh + 1) * LRU_BLOCK]
        rs.append(jnp.dot(xb, wr_ref[hh].astype(BF16), preferred_element_type=F32))
        gs.append(jnp.dot(xb, wi_ref[hh].astype(BF16), preferred_element_type=F32))
    r = jax.nn.sigmoid(jnp.concatenate(rs, axis=-1) + br_ref[...])
    ig = jax.nn.sigmoid(jnp.concatenate(gs, axis=-1) + bi_ref[...])
    log_a = -LRU_C * r * _softplus(-lam_ref[...])
    a = jnp.exp(log_a)
    u = jnp.sqrt(-(jnp.tanh(log_a) * (jnp.exp(2.0 * log_a) + 1.0))) * (ig * xc)

    row = lax.broadcasted_iota(jnp.int32, (T, tc), 0)
    s = 1
    while s < T:
        keep = row >= s
        u = u + a * jnp.where(keep, pltpu.roll(u, s, 0), 0.0)
        a = a * jnp.where(keep, pltpu.roll(a, s, 0), 1.0)
        s *= 2
    h = u + a * hc_ref[...]
    hc_ref[...] = h[T - 1:T, :]
    y_ref[...] = (h * _gelu(ga_ref[...])).astype(BF16)

    @pl.when(t == nt - 1)
    def _():
        cst_ref[0] = buf_ref[T:T + HIST_ROWS, :]
        hl_ref[0] = h[T - 1:T, :]


def _lru_call(proj, y_prev, hist, h0, p, e, *, n_seq, L, T, row0, m):
    nt = L // T
    rb0 = row0 // T
    tc = LRU_TILE
    nc = LRU_WIDTH // tc
    hpt = tc // LRU_BLOCK
    row = lambda s, c, t: rb0 + s * nt + t
    vec = lambda: pl.BlockSpec((None, 1, tc), lambda s, c, t: (e, 0, c))
    in_specs = [
        pl.BlockSpec((T, tc), lambda s, c, t: (row(s, c, t), c)),
        pl.BlockSpec((T, tc), lambda s, c, t: (row(s, c, t), nc + c)),
        pl.BlockSpec((1, HIST_ROWS, tc), lambda s, c, t: (s, 0, c)),
        pl.BlockSpec((1, 1, tc), lambda s, c, t: (s, 0, c)),
        pl.BlockSpec((None, LRU_CONV, tc), lambda s, c, t: (e, 0, c)),
        vec(),
        pl.BlockSpec((None, hpt, LRU_BLOCK, LRU_BLOCK), lambda s, c, t: (e, c, 0, 0)),
        vec(),
        pl.BlockSpec((None, hpt, LRU_BLOCK, LRU_BLOCK), lambda s, c, t: (e, c, 0, 0)),
        vec(),
        vec(),
    ]
    args = [proj, proj, hist, h0, p["lru_conv_w"], p["lru_conv_b"], p["lru_w_r"], p["lru_b_r"], p["lru_w_i"], p["lru_b_i"],
            p["lru_lambda"]]
    aliases = {}
    if y_prev is not None:
        in_specs.append(pl.BlockSpec(memory_space=pl.ANY))
        args.append(y_prev)
        aliases = {len(args) - 1: 0}
    return pl.pallas_call(
        functools.partial(_lru_kernel, T=T, nt=nt, aliased=y_prev is not None),
        name=f"lru_L{L}",
        out_shape=(jax.ShapeDtypeStruct((m, D_MODEL), BF16),
                   jax.ShapeDtypeStruct((n_seq, HIST_ROWS, LRU_WIDTH), F32),
                   jax.ShapeDtypeStruct((n_seq, 1, LRU_WIDTH), F32)),
        grid=(n_seq, nc, nt),
        in_specs=in_specs,
        out_specs=(pl.BlockSpec((T, tc), lambda s, c, t: (row(s, c, t), c)),
                   pl.BlockSpec((1, HIST_ROWS, tc), lambda s, c, t: (s, 0, c)),
                   pl.BlockSpec((1, 1, tc), lambda s, c, t: (s, 0, c))),
        scratch_shapes=[pltpu.VMEM((T + HIST_ROWS, tc), F32), pltpu.VMEM((1, tc), F32)],
        input_output_aliases=aliases,
        compiler_params=pltpu.CompilerParams(dimension_semantics=("arbitrary", "arbitrary", "arbitrary"),
                                             vmem_limit_bytes=VMEM_LIMIT_SMALL),
    )(*args)


def _ssd_kernel(*refs, T, nt):
    (z_ref, x_ref, b_ref, c_ref, dt_ref, hx_ref, hb_ref, hcn_ref, s0_ref,
     cwx_ref, cbx_ref, cwb_ref, cbb_ref, cwc_ref, cbc_ref, alog_ref, dd_ref, nw_ref, _y_prev,
     y_ref, csx_ref, csb_ref, csc_ref, sf_ref, bufx_ref, bufb_ref, bufc_ref, s_ref, yd_ref) = refs
    t = pl.program_id(2)
    P = SSD_HEAD_DIM
    gw, n = GROUP_WIDTH, SSD_STATE
    width = x_ref.shape[1]

    xs_all = _silu(_conv_step(bufx_ref, x_ref[...], hx_ref, cwx_ref, cbx_ref, SSD_CONV, t, T))
    bm_all = _silu(_conv_step(bufb_ref, b_ref[...], hb_ref, cwb_ref, cbb_ref, SSD_CONV, t, T))
    cm_all = _silu(_conv_step(bufc_ref, c_ref[...], hcn_ref, cwc_ref, cbc_ref, SSD_CONV, t, T))

    @pl.when(t == 0)
    def _():
        s_ref[...] = s0_ref[0]

    dt = dt_ref[...]
    adt = -jnp.exp(alog_ref[...]) * dt
    row = lax.broadcasted_iota(jnp.int32, (T, width), 0)
    acs_all = adt
    s = 1
    while s < T:
        acs_all = acs_all + jnp.where(row >= s, pltpu.roll(acs_all, s, 0), 0.0)
        s *= 2
    xd_all = xs_all * dt
    eacs_all = jnp.exp(acs_all)
    last_all = acs_all[T - 1:T, :]
    xdd_all = xd_all * jnp.exp(last_all - acs_all)
    elast_all = jnp.exp(last_all)
    tri = lax.broadcasted_iota(jnp.int32, (T, T), 0) >= lax.broadcasted_iota(jnp.int32, (T, T), 1)

    for gg in range(SSD_GROUPS_PER_STEP):
        cols = slice(gg * gw, (gg + 1) * gw)
        acs = acs_all[:, cols]
        xd = xd_all[:, cols]
        bmb = bm_all[:, gg * n:(gg + 1) * n].astype(BF16)
        cmb = cm_all[:, gg * n:(gg + 1) * n].astype(BF16)
        cb = lax.dot_general(cmb, bmb, (((1,), (1,)), ((), ())), preferred_element_type=F32)
        for half in range(gw // LANES):
            blk = acs[:, half * LANES:(half + 1) * LANES]
            if T < LANES:
                blk = jnp.concatenate([blk, jnp.zeros((LANES - T, LANES), F32)], axis=0)
            blk_t = blk.T
            for rr in range(LANES // P):
                r = half * (LANES // P) + rr
                col = acs[:, r * P:r * P + 1]
                rowv = blk_t[rr * P:rr * P + 1, :T]
                decay = jnp.exp(jnp.where(tri, col - rowv, -jnp.inf))
                md = (cb * decay).astype(BF16)
                yd_ref[:, gg * gw + r * P:gg * gw + (r + 1) * P] = jnp.dot(
                    md, xd[:, r * P:(r + 1) * P].astype(BF16), preferred_element_type=F32)

        st = s_ref[gg]
        y_off = jnp.dot(cmb, st.astype(BF16), preferred_element_type=F32) * eacs_all[:, cols]
        st_new = st * elast_all[:, cols] + lax.dot_general(
            bmb, xdd_all[:, cols].astype(BF16), (((0,), (0,)), ((), ())), preferred_element_type=F32)
        s_ref[gg] = st_new
        yt = yd_ref[:, cols] + y_off + dd_ref[:, cols] * xs_all[:, cols]
        yt = yt * _silu(z_ref[:, cols])
        yt = yt * lax.rsqrt(jnp.mean(yt * yt, axis=-1, keepdims=True) + NORM_EPS) * nw_ref[:, cols]
        y_ref[:, cols] = yt.astype(BF16)

    @pl.when(t == nt - 1)
    def _():
        csx_ref[0] = bufx_ref[T:T + HIST_ROWS, :]
        csb_ref[0] = bufb_ref[T:T + HIST_ROWS, :]
        csc_ref[0] = bufc_ref[T:T + HIST_ROWS, :]
        sf_ref[0] = s_ref[...]


def _ssd_call(proj, dtr, y_prev, hist, s0, p, e, *, n_seq, L, T, row0, m):
    nt = L // T
    rb0 = row0 // T
    gs = SSD_GROUPS_PER_STEP
    G = SSD_GROUPS
    gw = gs * GROUP_WIDTH
    n = gs * SSD_STATE
    row = lambda s, g, t: rb0 + s * nt + t
    zb = OFF_SSD_Z // gw
    xb = OFF_SSD_XBC // gw
    bb = (OFF_SSD_XBC + SSD_INNER) // n
    cb = (OFF_SSD_XBC + SSD_INNER + SSD_BC) // n
    hb = SSD_INNER // n
    hc = (SSD_INNER + SSD_BC) // n
    vec = lambda: pl.BlockSpec((None, 1, gw), lambda s, g, t: (e, 0, g))
    in_specs = [
        pl.BlockSpec((T, gw), lambda s, g, t: (row(s, g, t), zb + g)),
        pl.BlockSpec((T, gw), lambda s, g, t: (row(s, g, t), xb + g)),
        pl.BlockSpec((T, n), lambda s, g, t: (row(s, g, t), bb + g)),
        pl.BlockSpec((T, n), lambda s, g, t: (row(s, g, t), cb + g)),
        pl.BlockSpec((T, gw), lambda s, g, t: (row(s, g, t), g)),
        pl.BlockSpec((1, HIST_ROWS, gw), lambda s, g, t: (s, 0, g)),
        pl.BlockSpec((1, HIST_ROWS, n), lambda s, g, t: (s, 0, hb + g)),
        pl.BlockSpec((1, HIST_ROWS, n), lambda s, g, t: (s, 0, hc + g)),
        pl.BlockSpec((1, gs, SSD_STATE, GROUP_WIDTH), lambda s, g, t: (s, g, 0, 0)),
        pl.BlockSpec((None, SSD_CONV, gw), lambda s, g, t: (e, 0, g)),
        pl.BlockSpec((None, 1, gw), lambda s, g, t: (e, 0, g)),
        pl.BlockSpec((None, SSD_CONV, n), lambda s, g, t: (e, 0, hb + g)),
        pl.BlockSpec((None, 1, n), lambda s, g, t: (e, 0, hb + g)),
        pl.BlockSpec((None, SSD_CONV, n), lambda s, g, t: (e, 0, hc + g)),
        pl.BlockSpec((None, 1, n), lambda s, g, t: (e, 0, hc + g)),
        vec(), vec(), vec(),
        pl.BlockSpec(memory_space=pl.ANY),
    ]
    args = [proj, proj, proj, proj, dtr, hist, hist, hist, s0,
            p["ssd_conv_w"], p["ssd_conv_b"], p["ssd_conv_w"], p["ssd_conv_b"], p["ssd_conv_w"], p["ssd_conv_b"],
            p["ssd_a_log_x"], p["ssd_d_x"], p["ssd_norm_w"], y_prev]
    yb = LRU_WIDTH // gw
    return pl.pallas_call(
        functools.partial(_ssd_kernel, T=T, nt=nt),
        name=f"ssd_L{L}",
        out_shape=(jax.ShapeDtypeStruct((m, D_MODEL), BF16),
                   jax.ShapeDtypeStruct((n_seq, HIST_ROWS, SSD_INNER), F32),
                   jax.ShapeDtypeStruct((n_seq, HIST_ROWS, SSD_BC), F32),
                   jax.ShapeDtypeStruct((n_seq, HIST_ROWS, SSD_BC), F32),
                   jax.ShapeDtypeStruct((n_seq, G, SSD_STATE, GROUP_WIDTH), F32)),
        grid=(n_seq, G // gs, nt),
        in_specs=in_specs,
        out_specs=(pl.BlockSpec((T, gw), lambda s, g, t: (row(s, g, t), yb + g)),
                   pl.BlockSpec((1, HIST_ROWS, gw), lambda s, g, t: (s, 0, g)),
                   pl.BlockSpec((1, HIST_ROWS, n), lambda s, g, t: (s, 0, g)),
                   pl.BlockSpec((1, HIST_ROWS, n), lambda s, g, t: (s, 0, g)),
                   pl.BlockSpec((1, gs, SSD_STATE, GROUP_WIDTH), lambda s, g, t: (s, g, 0, 0))),
        scratch_shapes=[pltpu.VMEM((T + HIST_ROWS, gw), F32), pltpu.VMEM((T + HIST_ROWS, n), F32),
                        pltpu.VMEM((T + HIST_ROWS, n), F32), pltpu.VMEM((gs, SSD_STATE, GROUP_WIDTH), F32),
                        pltpu.VMEM((T, gw), F32)],
        input_output_aliases={len(args) - 1: 0},
        compiler_params=pltpu.CompilerParams(dimension_semantics=("arbitrary", "arbitrary", "arbitrary"),
                                             vmem_limit_bytes=VMEM_LIMIT_SMALL),
    )(*args)


def _ret_kernel(*refs, nt, n_aliased):
    (q_ref, k_ref, v_ref, g_ref, cos_ref, sin_ref, dm_ref, cr_ref, kd_ref, cd_ref, s0_ref) = refs[:11]
    y_ref, sf_ref, s_ref = refs[11 + n_aliased:]
    t = pl.program_id(2)
    half = RET_QK_DIM // 2
    dk, dv = RET_QK_DIM, RET_V_DIM

    @pl.when(t == 0)
    def _():
        s_ref[...] = s0_ref[0]

    cos = cos_ref[...]
    sin = sin_ref[...]

    def rot(x):
        x1 = x[:, :half]
        x2 = x[:, half:]
        return jnp.concatenate([x1 * cos - x2 * sin, x2 * cos + x1 * sin], axis=-1)

    for hh in range(RET_HEADS_PER_STEP):
        q = rot(q_ref[:, hh * dk:(hh + 1) * dk].astype(F32))
        k = rot(k_ref[:, hh * dk:(hh + 1) * dk].astype(F32)) * (RET_QK_DIM ** -0.5)
        qb = q.astype(BF16)
        vb = v_ref[:, hh * dv:(hh + 1) * dv].astype(BF16)
        att = lax.dot_general(qb, k.astype(BF16), (((1,), (1,)), ((), ())), preferred_element_type=F32) * dm_ref[hh]
        st = s_ref[hh]
        o = (jnp.dot(att.astype(BF16), vb, preferred_element_type=F32)
             + jnp.dot((q * cr_ref[hh]).astype(BF16), st.astype(BF16), preferred_element_type=F32))
        st_new = st * cd_ref[hh][0:1, :] + lax.dot_general((k * kd_ref[hh]).astype(BF16), vb, (((0,), (0,)), ((), ())),
                                                            preferred_element_type=F32)
        s_ref[hh] = st_new
        o = o * lax.rsqrt(jnp.mean(o * o, axis=-1, keepdims=True) + NORM_EPS)
        y_ref[:, hh * dv:(hh + 1) * dv] = (_silu(g_ref[:, hh * dv:(hh + 1) * dv].astype(F32)) * o).astype(BF16)

    @pl.when(t == nt - 1)
    def _():
        sf_ref[0] = s_ref[...]


def _ret_tables(T, ct, pos0, L):
    f32 = F32
    log_gamma = jnp.log1p(-jnp.exp2(-5.0 - jnp.arange(RET_HEADS, dtype=f32)))
    n = jnp.arange(T, dtype=f32)
    chunk = jnp.arange(T) // ct
    same = chunk[:, None] == chunk[None, :]
    earlier = chunk[None, :] < chunk[:, None]
    diff = n[:, None] - n[None, :]
    lg = log_gamma[:, None, None]
    dm = jnp.where(same[None], jnp.exp(jnp.abs(diff)[None] * lg), jnp.where(earlier[None], jnp.exp(diff[None] * lg), 0.0))
    cross = jnp.exp((n[None, :] + 1.0) * log_gamma[:, None])
    kvd = jnp.exp((T - 1.0 - n)[None, :] * log_gamma[:, None])
    cdec = jnp.exp(T * log_gamma)
    cr = jnp.broadcast_to(cross[:, :, None], (RET_HEADS, T, RET_QK_DIM))
    kd = jnp.broadcast_to(kvd[:, :, None], (RET_HEADS, T, RET_QK_DIM))
    cd = jnp.broadcast_to(cdec[:, None, None], (RET_HEADS, SUBLANES, RET_V_DIM))
    half = RET_QK_DIM // 2
    inv = 1.0 / (ROPE_BASE ** jnp.linspace(0.0, 1.0, half, dtype=f32))
    pos = float(pos0) + jnp.arange(L, dtype=f32)
    ang = pos[:, None] * inv[None, :]
    return dm, cr, kd, cd, jnp.cos(ang), jnp.sin(ang)


def _ret_call(proj, y_prev, s0, e0, sf_prev, e, n_layers, *, n_seq, L, T, row0, pos0, m):
    nt = L // T
    rb0 = row0 // T
    ct = min(CHUNK, L)
    assert T % ct == 0
    dm, cr, kd, cd, cos, sin = _ret_tables(T, ct, pos0, L)
    H = RET_HEADS
    hs = RET_HEADS_PER_STEP
    dk, dv = hs * RET_QK_DIM, hs * RET_V_DIM
    row = lambda s, h, t: rb0 + s * nt + t
    kb0 = RET_QK_WIDTH // dk
    vb0 = 2 * RET_QK_WIDTH // dv
    gb0 = (2 * RET_QK_WIDTH + RET_V_WIDTH) // dv
    in_specs = [
        pl.BlockSpec((T, dk), lambda s, h, t: (row(s, h, t), h)),
        pl.BlockSpec((T, dk), lambda s, h, t: (row(s, h, t), kb0 + h)),
        pl.BlockSpec((T, dv), lambda s, h, t: (row(s, h, t), vb0 + h)),
        pl.BlockSpec((T, dv), lambda s, h, t: (row(s, h, t), gb0 + h)),
        pl.BlockSpec((T, RET_QK_DIM // 2), lambda s, h, t: (t, 0)),
        pl.BlockSpec((T, RET_QK_DIM // 2), lambda s, h, t: (t, 0)),
        pl.BlockSpec((hs, T, T), lambda s, h, t: (h, 0, 0)),
        pl.BlockSpec((hs, T, RET_QK_DIM), lambda s, h, t: (h, 0, 0)),
        pl.BlockSpec((hs, T, RET_QK_DIM), lambda s, h, t: (h, 0, 0)),
        pl.BlockSpec((hs, SUBLANES, RET_V_DIM), lambda s, h, t: (h, 0, 0)),
        pl.BlockSpec((None, 1, hs, RET_QK_DIM, RET_V_DIM), lambda s, h, t: (e0, s, h, 0, 0)),
    ]
    args = [proj, proj, proj, proj, cos, sin, dm, cr, kd, cd, s0]
    aliases = {}
    for out_idx, prev in ((0, y_prev), (1, sf_prev)):
        if prev is not None:
            in_specs.append(pl.BlockSpec(memory_space=pl.ANY))
            args.append(prev)
            aliases[len(args) - 1] = out_idx
    return pl.pallas_call(
        functools.partial(_ret_kernel, nt=nt, n_aliased=len(aliases)),
        name=f"ret_L{L}",
        out_shape=(jax.ShapeDtypeStruct((m, RET_V_WIDTH), BF16),
                   jax.ShapeDtypeStruct((n_layers, n_seq, H, RET_QK_DIM, RET_V_DIM), F32)),
        grid=(n_seq, H // hs, nt),
        in_specs=in_specs,
        out_specs=(pl.BlockSpec((T, dv), lambda s, h, t: (row(s, h, t), h)),
                   pl.BlockSpec((None, 1, hs, RET_QK_DIM, RET_V_DIM), lambda s, h, t: (e, s, h, 0, 0))),
        scratch_shapes=[pltpu.VMEM((hs, RET_QK_DIM, RET_V_DIM), F32)],
        input_output_aliases=aliases,
        compiler_params=pltpu.CompilerParams(dimension_semantics=("arbitrary", "arbitrary", "arbitrary"),
                                             vmem_limit_bytes=VMEM_LIMIT_SMALL),
    )(*args)


def _pad_hist(state, n_zero_seq):
    n, km1, c = state.shape
    padded = jnp.pad(state, ((n_zero_seq, 0), (HIST_ROWS - km1, 0), (0, 0)))
    return padded


def _ssd_state_in(s):
    n = s.shape[0]
    s = s.reshape(n, SSD_GROUPS, HEADS_PER_GROUP, SSD_HEAD_DIM, SSD_STATE)
    return s.transpose(0, 1, 4, 2, 3).reshape(n, SSD_GROUPS, SSD_STATE, GROUP_WIDTH)


def _ssd_state_out(s):
    n = s.shape[0]
    s = s.reshape(n, SSD_GROUPS, SSD_STATE, HEADS_PER_GROUP, SSD_HEAD_DIM)
    return s.transpose(0, 1, 3, 4, 2).reshape(n, SSD_HEADS, SSD_HEAD_DIM, SSD_STATE)


def kernel(x_prompt, x_sample, state_lru_conv, state_lru_h, state_ssd_conv, state_ssd, state_ret, state_ffn_conv,
           norm_mix_pre, norm_mix_post, norm_ffn_pre, norm_ffn_post, w_in_ab, lru_conv_w, lru_conv_b, lru_w_r, lru_b_r,
           lru_w_i, lru_b_i, lru_lambda, ssd_conv_w, ssd_conv_b, ssd_dt_bias, ssd_a_log, ssd_d, ssd_norm_w, w_out_ab,
           w_in_ret, w_out_ret, ffn_w_up, ffn_conv_w, ffn_conv_b, ffn_w_down):
    bp, lp, d = x_prompt.shape
    bs, ls, _ = x_sample.shape
    depth = norm_mix_pre.shape[0]
    n_even = w_in_ab.shape[0]
    mp = bp * lp
    m = mp + bs * ls

    groups = ((bp, lp, 0, 0), (bs, ls, mp, PAST_LEN))
    t_lru = (min(256, lp), ls)
    t_ssd = (min(128, lp), ls)
    t_ret = (min(256, lp), ls)

    rep = lambda v: jnp.repeat(v, SSD_HEAD_DIM, axis=-1).reshape(n_even, 1, SSD_INNER)
    vec3 = lambda v: v.reshape(v.shape[0], 1, v.shape[-1])
    ab = {
        "lru_conv_w": lru_conv_w, "lru_conv_b": vec3(lru_conv_b), "lru_w_r": lru_w_r, "lru_b_r": vec3(lru_b_r),
        "lru_w_i": lru_w_i, "lru_b_i": vec3(lru_b_i), "lru_lambda": vec3(lru_lambda),
        "ssd_conv_w": ssd_conv_w, "ssd_conv_b": vec3(ssd_conv_b),
        "ssd_a_log_x": rep(ssd_a_log), "ssd_d_x": rep(ssd_d),
        "ssd_norm_w": vec3(ssd_norm_w),
    }
    ffn_hist = _pad_hist(state_ffn_conv.reshape(depth * bs, FFN_CONV - 1, D_FF), 0).reshape(depth, bs, HIST_ROWS, D_FF)
    ffn_hist = jnp.pad(ffn_hist, ((0, 0), (bp, 0), (0, 0), (0, 0)))

    new = {k: ([], []) for k in ("lc", "lh", "sc", "ss", "fc")}
    ret_stack = [None, None]
    w_in_ab_t = jnp.swapaxes(w_in_ab, 1, 2)

    x, hn = _first_norm(x_prompt.reshape(mp, d), x_sample.reshape(bs * ls, d), norm_mix_pre[0])
    for layer in range(depth):
        e = layer // 2
        if layer % 2 == 0:
            proj = _matmul(hn, w_in_ab_t, e, n_cols=OFF_SSD_DT, w_transposed=True)
            dtr = _dt_proj(hn, w_in_ab_t, ssd_dt_bias, e)
            y = None
            for gi, (n_seq, L, row0, _) in enumerate(groups):
                if gi == 0:
                    hist = jnp.zeros((n_seq, HIST_ROWS, LRU_WIDTH), F32)
                    h0 = jnp.zeros((n_seq, 1, LRU_WIDTH), F32)
                else:
                    hist = _pad_hist(state_lru_conv[e], 0)
                    h0 = state_lru_h[e].reshape(n_seq, 1, LRU_WIDTH)
                y, cst, hl = _lru_call(proj, y, hist, h0, ab, e, n_seq=n_seq, L=L, T=t_lru[gi], row0=row0, m=m)
                new["lc"][gi].append(cst[:, HIST_ROWS - (LRU_CONV - 1):, :])
                new["lh"][gi].append(hl.reshape(n_seq, LRU_WIDTH))
            for gi, (n_seq, L, row0, _) in enumerate(groups):
                if gi == 0:
                    hist = jnp.zeros((n_seq, HIST_ROWS, SSD_CONV_DIM), F32)
                    s0 = jnp.zeros((n_seq, SSD_GROUPS, SSD_STATE, GROUP_WIDTH), F32)
                else:
                    hist = _pad_hist(state_ssd_conv[e], 0)
                    s0 = _ssd_state_in(state_ssd[e])
                y, csx, csb, csc, sf = _ssd_call(proj, dtr, y, hist, s0, ab, e, n_seq=n_seq, L=L, T=t_ssd[gi], row0=row0, m=m)
                cst = jnp.concatenate([csx, csb, csc], axis=-1)
                new["sc"][gi].append(cst[:, HIST_ROWS - (SSD_CONV - 1):, :])
                new["ss"][gi].append(_ssd_state_out(sf))
            mix = _matmul(y, w_out_ab, e, out_dtype=BF16)
        else:
            proj = _matmul(hn, w_in_ret, e, out_dtype=BF16)
            y = None
            for gi, (n_seq, L, row0, pos0) in enumerate(groups):
                if gi == 0:
                    s0, e0 = jnp.zeros((1, n_seq, RET_HEADS, RET_QK_DIM, RET_V_DIM), F32), 0
                else:
                    s0, e0 = state_ret, e
                y, ret_stack[gi] = _ret_call(proj, y, s0, e0, ret_stack[gi], e, depth // 2, n_seq=n_seq, L=L, T=t_ret[gi],
                                             row0=row0, pos0=pos0, m=m)
            mix = _matmul(y, w_out_ret, e, out_dtype=BF16)
        x, hn = _row_call(_resid_norm_kernel, [x, mix], [norm_mix_post[layer], norm_ffn_pre[layer]], [F32, BF16])
        h, fst = _ffn_up(hn, ffn_w_up, ffn_conv_w, ffn_conv_b, ffn_hist[layer], layer, bp=bp, lp=lp, bs=bs, ls=ls)
        fst = fst[:, HIST_ROWS - (FFN_CONV - 1):, :]
        new["fc"][0].append(fst[:bp])
        new["fc"][1].append(fst[bp:])
        f = _matmul(h, ffn_w_down, layer, out_dtype=BF16)
        if layer + 1 < depth:
            x, hn = _row_call(_resid_norm_kernel, [x, f], [norm_ffn_post[layer], norm_mix_pre[layer + 1]], [F32, BF16])
        else:
            y_prompt, y_sample = _last_resid(x, f, norm_ffn_post[layer], mp)

    outs = [y_prompt.reshape(bp, lp, d), y_sample.reshape(bs, ls, d)]
    for gi in range(2):
        for k in ("lc", "lh", "sc", "ss", "rs", "fc"):
            outs.append(ret_stack[gi] if k == "rs" else jnp.stack(new[k][gi]))
    return tuple(outs)
```

```python
import functools

import jax
import jax.numpy as jnp
from jax import lax
from jax.experimental import pallas as pl
from jax.experimental.pallas import tpu as pltpu

F32 = jnp.float32
BF16 = jnp.bfloat16

D_MODEL = 4096
PAST_LEN = 1024
CHUNK = 64
NORM_EPS = 1e-6
LRU_WIDTH = D_MODEL // 2
LRU_HEADS = 16
LRU_BLOCK = LRU_WIDTH // LRU_HEADS
LRU_CONV = 4
LRU_C = 8.0
SSD_INNER = D_MODEL // 2
SSD_HEAD_DIM = 64
SSD_HEADS = SSD_INNER // SSD_HEAD_DIM
SSD_GROUPS = 8
SSD_STATE = 128
SSD_CONV = 4
SSD_BC = SSD_GROUPS * SSD_STATE
SSD_CONV_DIM = SSD_INNER + 2 * SSD_BC
OFF_SSD_Z = 2 * LRU_WIDTH
OFF_SSD_XBC = OFF_SSD_Z + SSD_INNER
OFF_SSD_DT = OFF_SSD_XBC + SSD_CONV_DIM
RET_HEADS = 16
RET_QK_DIM = D_MODEL // RET_HEADS
RET_V_DIM = 2 * RET_QK_DIM
RET_QK_WIDTH = RET_HEADS * RET_QK_DIM
RET_V_WIDTH = RET_HEADS * RET_V_DIM
ROPE_BASE = 10000.0
D_FF = 3 * D_MODEL
FFN_CONV = 3

SUBLANES = 8
LANES = 128
BF16_ROWS = 16
HIST_ROWS = SUBLANES
MIB = 1 << 20
VMEM_LIMIT_BIG = 56 * MIB
VMEM_LIMIT_SMALL = 40 * MIB

HEADS_PER_GROUP = SSD_HEADS // SSD_GROUPS
GROUP_WIDTH = HEADS_PER_GROUP * SSD_HEAD_DIM
LRU_TILE = 512
MATMUL_TM_CAP = 1056
MATMUL_TK = 4096
MATMUL_TN = 1024
FFN_TILE = 1024
FFN_COLS = 512
RET_HEADS_PER_STEP = 4
SSD_GROUPS_PER_STEP = 4


def _pick_tile(total, cap, mult):
    best = None
    for t in range(mult, min(total, cap) + 1, mult):
        if total % t == 0:
            best = t
    assert best is not None, (total, cap, mult)
    return best


def _silu(x):
    return x * jax.nn.sigmoid(x)


def _gelu(x):
    return jax.nn.gelu(x, approximate=True)


def _softplus(x):
    return jnp.maximum(x, 0.0) + jnp.log1p(jnp.exp(-jnp.abs(x)))


def _rms(x, w):
    return x * lax.rsqrt(jnp.mean(x * x, axis=-1, keepdims=True) + NORM_EPS) * w


def _resid_norm_kernel(x_ref, m_ref, wp_ref, wn_ref, xo_ref, hn_ref):
    xn = x_ref[...] + _rms(m_ref[...].astype(F32), wp_ref[...])
    xo_ref[...] = xn
    hn_ref[...] = _rms(xn, wn_ref[...]).astype(BF16)


def _first_norm_kernel(xp_ref, xs_ref, w_ref, xo_ref, hn_ref, *, n_ptiles):
    i = pl.program_id(0)

    def emit(x):
        xo_ref[...] = x
        hn_ref[...] = _rms(x, w_ref[...]).astype(BF16)

    @pl.when(i < n_ptiles)
    def _():
        emit(xp_ref[...])

    @pl.when(i >= n_ptiles)
    def _():
        emit(xs_ref[...])


def _last_resid_kernel(x_ref, m_ref, wp_ref, yp_ref, ys_ref, *, n_ptiles):
    i = pl.program_id(0)
    xn = x_ref[...] + _rms(m_ref[...].astype(F32), wp_ref[...])

    @pl.when(i < n_ptiles)
    def _():
        yp_ref[...] = xn

    @pl.when(i >= n_ptiles)
    def _():
        ys_ref[...] = xn


def _row_tile(mp, ms):
    tr = _pick_tile(mp, 256, BF16_ROWS)
    assert ms % tr == 0
    return tr


def _first_norm(xp, xs, w):
    mp, d = xp.shape
    ms = xs.shape[0]
    tr = _row_tile(mp, ms)
    n_ptiles = mp // tr
    row_spec = pl.BlockSpec((tr, d), lambda i: (i, 0))
    return pl.pallas_call(
        functools.partial(_first_norm_kernel, n_ptiles=n_ptiles),
        name="first_norm",
        out_shape=(jax.ShapeDtypeStruct((mp + ms, d), F32), jax.ShapeDtypeStruct((mp + ms, d), BF16)),
        grid=((mp + ms) // tr,),
        in_specs=[pl.BlockSpec((tr, d), lambda i: (jnp.minimum(i, n_ptiles - 1), 0)),
                  pl.BlockSpec((tr, d), lambda i: (jnp.maximum(i - n_ptiles, 0), 0)),
                  pl.BlockSpec((1, d), lambda i: (0, 0))],
        out_specs=(row_spec, row_spec),
        compiler_params=pltpu.CompilerParams(dimension_semantics=("arbitrary",), vmem_limit_bytes=VMEM_LIMIT_SMALL),
    )(xp, xs, w.reshape(1, d))


def _last_resid(x, mix, w, mp):
    m, d = x.shape
    ms = m - mp
    tr = _row_tile(mp, ms)
    n_ptiles = mp // tr
    row_spec = pl.BlockSpec((tr, d), lambda i: (i, 0))
    return pl.pallas_call(
        functools.partial(_last_resid_kernel, n_ptiles=n_ptiles),
        name="last_resid",
        out_shape=(jax.ShapeDtypeStruct((mp, d), F32), jax.ShapeDtypeStruct((ms, d), F32)),
        grid=(m // tr,),
        in_specs=[row_spec, row_spec, pl.BlockSpec((1, d), lambda i: (0, 0))],
        out_specs=(pl.BlockSpec((tr, d), lambda i: (jnp.minimum(i, n_ptiles - 1), 0)),
                   pl.BlockSpec((tr, d), lambda i: (jnp.maximum(i - n_ptiles, 0), 0))),
        compiler_params=pltpu.CompilerParams(dimension_semantics=("arbitrary",), vmem_limit_bytes=VMEM_LIMIT_SMALL),
    )(x, mix, w.reshape(1, d))


def _row_call(kernel_fn, arrays, weights, out_dtypes):
    m, d = arrays[0].shape
    tr = _pick_tile(m, 256, BF16_ROWS)
    row_spec = pl.BlockSpec((tr, d), lambda i: (i, 0))
    w_spec = pl.BlockSpec((1, d), lambda i: (0, 0))
    outs = tuple(jax.ShapeDtypeStruct((m, d), dt) for dt in out_dtypes)
    res = pl.pallas_call(
        kernel_fn,
        name=kernel_fn.__name__.strip("_"),
        out_shape=outs,
        grid=(m // tr,),
        in_specs=[row_spec] * len(arrays) + [w_spec] * len(weights),
        out_specs=tuple(row_spec for _ in outs),
        compiler_params=pltpu.CompilerParams(dimension_semantics=("arbitrary",), vmem_limit_bytes=VMEM_LIMIT_SMALL),
    )(*arrays, *[w.reshape(1, d) for w in weights])
    return res


def _matmul_kernel(a_ref, w_ref, o_ref, wb0_ref, wb1_ref, *, nq, ck, w_transposed):
    q = pl.program_id(0)
    i = pl.program_id(1)
    wb_refs = (wb0_ref, wb1_ref)

    def stage(par):
        w = w_ref[...]
        if w_transposed:
            w = w.T
        wb_refs[par][pl.ds(pl.multiple_of(i * ck, BF16_ROWS), ck), :] = w.astype(BF16)

    def product(par):
        o_ref[...] = jnp.dot(a_ref[...], wb_refs[par][...], preferred_element_type=F32).astype(o_ref.dtype)

    @pl.when(q == 0)
    def _():
        stage(0)

    for par in (0, 1):
        @pl.when(jnp.logical_and(jnp.logical_and(q >= 1, q < nq), q % 2 == par))
        def _():
            stage(par)
            product(1 - par)

    @pl.when(q == nq)
    def _():
        product((nq - 1) % 2)


def _matmul(a, w, layer, *, n_cols=None, out_dtype=F32, w_transposed=False):
    m, k = a.shape
    n = n_cols if n_cols is not None else w.shape[-2 if w_transposed else -1]
    depth_ratio = max(1, k // MATMUL_TK)
    tm = _pick_tile(m, MATMUL_TM_CAP // depth_ratio, BF16_ROWS)
    tn = MATMUL_TN if depth_ratio == 1 else MATMUL_TN // 2
    n_mt = m // tm
    ck = k // n_mt
    assert n % tn == 0 and k % n_mt == 0 and ck % LANES == 0
    nq = n // tn

    def a_map(q, i):
        return (jnp.where(q >= 1, i, 0), 0)

    def w_map(q, i):
        row = jnp.where(q < nq, i, n_mt - 1)
        p = jnp.minimum(q, nq - 1)
        return (layer, p, row) if w_transposed else (layer, row, p)

    def o_map(q, i):
        return (jnp.where(q >= 1, i, 0), jnp.maximum(q - 1, 0))

    return pl.pallas_call(
        functools.partial(_matmul_kernel, nq=nq, ck=ck, w_transposed=w_transposed),
        name=f"matmul_k{k}_n{n}",
        out_shape=jax.ShapeDtypeStruct((m, n), out_dtype),
        grid=(nq + 1, n_mt),
        in_specs=[
            pl.BlockSpec((tm, k), a_map),
            pl.BlockSpec((None, tn, ck) if w_transposed else (None, ck, tn), w_map),
        ],
        out_specs=pl.BlockSpec((tm, tn), o_map),
        scratch_shapes=[pltpu.VMEM((k, tn), BF16), pltpu.VMEM((k, tn), BF16)],
        compiler_params=pltpu.CompilerParams(
            dimension_semantics=("arbitrary", "arbitrary"), vmem_limit_bytes=VMEM_LIMIT_BIG),
    )(a, w)


def _dt_kernel(hn_ref, w_ref, b_ref, o_ref):
    tm = hn_ref.shape[0]
    dt = lax.dot_general(hn_ref[...], w_ref[...].astype(BF16), (((1,), (1,)), ((), ())),
                         preferred_element_type=F32)
    dt = _softplus(dt + b_ref[...])
    for h in range(SSD_HEADS):
        o_ref[:, h * SSD_HEAD_DIM:(h + 1) * SSD_HEAD_DIM] = jnp.broadcast_to(dt[:, h:h + 1], (tm, SSD_HEAD_DIM))


def _dt_proj(hn, w_t, dt_bias, layer):
    m, d = hn.shape
    tm = _pick_tile(m, MATMUL_TM_CAP, BF16_ROWS)
    assert OFF_SSD_DT % SSD_HEADS == 0
    return pl.pallas_call(
        _dt_kernel,
        name="dt_proj",
        out_shape=jax.ShapeDtypeStruct((m, SSD_INNER), F32),
        grid=(m // tm,),
        in_specs=[pl.BlockSpec((tm, d), lambda i: (i, 0)),
                  pl.BlockSpec((None, SSD_HEADS, d), lambda i: (layer, OFF_SSD_DT // SSD_HEADS, 0)),
                  pl.BlockSpec((None, 1, SSD_HEADS), lambda i: (layer, 0, 0))],
        out_specs=pl.BlockSpec((tm, SSD_INNER), lambda i: (i, 0)),
        compiler_params=pltpu.CompilerParams(dimension_semantics=("arbitrary",), vmem_limit_bytes=VMEM_LIMIT_SMALL),
    )(hn, w_t, dt_bias.reshape(dt_bias.shape[0], 1, SSD_HEADS))


def _conv_from_buf(buf_ref, cw_ref, cb_ref, width, rows):
    v = buf_ref[0:HIST_ROWS + rows, :]

    def tap(k):
        back = width - 1 - k
        return v[HIST_ROWS:, :] if back == 0 else pltpu.roll(v, back, 0)[HIST_ROWS:, :]

    y = cb_ref[...] + tap(0) * cw_ref[0:1, :]
    for k in range(1, width):
        y = y + tap(k) * cw_ref[k:k + 1, :]
    return y


def _conv_step(buf_ref, src, hist_ref, cw_ref, cb_ref, width, t, rows):
    @pl.when(t == 0)
    def _():
        buf_ref[0:HIST_ROWS, :] = hist_ref[0]

    @pl.when(t > 0)
    def _():
        buf_ref[0:HIST_ROWS, :] = buf_ref[rows:rows + HIST_ROWS, :]

    buf_ref[HIST_ROWS:HIST_ROWS + rows, :] = src
    return _conv_from_buf(buf_ref, cw_ref, cb_ref, width, rows)


def _ffn_up_kernel(a_ref, wg_ref, wu_ref, cw_ref, cb_ref, hist_ref, h_ref, st_ref, wb0_ref, wb1_ref, buf_ref, *,
                   tiles_per_seq, n_ptiles, n_pseq, n_sseq, ls, tf, nq):
    q = pl.program_id(0)
    i = pl.program_id(1)
    tm = a_ref.shape[0]
    ck = wg_ref.shape[0]
    wb_refs = (wb0_ref, wb1_ref)
    H = HIST_ROWS

    def stage(par):
        rows = pl.ds(pl.multiple_of(i * ck, BF16_ROWS), ck)
        wb_refs[par][rows, 0:tf] = wg_ref[...].astype(BF16)
        wb_refs[par][rows, tf:2 * tf] = wu_ref[...].astype(BF16)

    def prompt_tile(par):
        p = jnp.dot(a_ref[...], wb_refs[par][...], preferred_element_type=F32)
        seq = i // tiles_per_seq
        first = (i % tiles_per_seq) == 0
        buf_ref[0:H, :] = jnp.where(first, hist_ref[seq], buf_ref[tm:tm + H, :])
        buf_ref[H:H + tm, :] = p[:, 0:tf]
        gate = _gelu(_conv_from_buf(buf_ref, cw_ref, cb_ref, FFN_CONV, tm))
        h_ref[...] = (gate * p[:, tf:2 * tf]).astype(BF16)
        st_ref[seq] = buf_ref[tm:tm + H, :]

    def sample_tile(par):
        rows = n_sseq * ls
        p = jnp.dot(a_ref[0:rows, :], wb_refs[par][...], preferred_element_type=F32)
        for s in range(n_sseq):
            lo = s * ls
            buf_ref[0:H, :] = hist_ref[n_pseq + s]
            buf_ref[H:H + ls, :] = p[lo:lo + ls, 0:tf]
            gate = _gelu(_conv_from_buf(buf_ref, cw_ref, cb_ref, FFN_CONV, ls))
            h_ref[lo:lo + ls, :] = (gate * p[lo:lo + ls, tf:2 * tf]).astype(BF16)
            st_ref[n_pseq + s] = buf_ref[ls:ls + H, :]

    is_prompt = i < n_ptiles

    @pl.when(jnp.logical_and(q == 0, i == 0))
    def _():
        buf_ref[tm:tm + H, :] = jnp.zeros((H, tf), F32)

    @pl.when(jnp.logical_and(q == 0, is_prompt))
    def _():
        stage(0)

    for par in (0, 1):
        @pl.when(jnp.logical_and(jnp.logical_and(q >= 1, q < nq), jnp.logical_and(q % 2 == par, is_prompt)))
        def _():
            stage(par)
            prompt_tile(1 - par)

        @pl.when(jnp.logical_and(jnp.logical_and(q >= 1, (q - 1) % 2 == par), i == n_ptiles))
        def _():
            sample_tile(par)

    @pl.when(jnp.logical_and(q == nq, is_prompt))
    def _():
        prompt_tile((nq - 1) % 2)


def _ffn_up(hn, w_up, conv_w, conv_b, hist, layer, *, bp, lp, bs, ls):
    m, d = hn.shape
    tf = FFN_COLS
    tm = min(FFN_TILE, lp)
    n_ptiles = bp * lp // tm
    ck = d // n_ptiles
    assert lp % tm == 0 and bs * ls <= tm and ls % BF16_ROWS == 0 and d % n_ptiles == 0 and ck % BF16_ROWS == 0
    nseq = bp + bs
    nq = D_FF // tf
    kern = functools.partial(_ffn_up_kernel, tiles_per_seq=lp // tm, n_ptiles=n_ptiles, n_pseq=bp, n_sseq=bs, ls=ls,
                             tf=tf, nq=nq)
    prev = lambda q: jnp.maximum(q - 1, 0)

    def w_row(q, i):
        return jnp.where(q < nq, jnp.minimum(i, n_ptiles - 1), n_ptiles - 1)

    return pl.pallas_call(
        kern,
        name="ffn_up",
        out_shape=(jax.ShapeDtypeStruct((m, D_FF), BF16), jax.ShapeDtypeStruct((nseq, HIST_ROWS, D_FF), F32)),
        grid=(nq + 1, n_ptiles + 1),
        in_specs=[
            pl.BlockSpec((tm, d), lambda q, i: (jnp.where(q >= 1, i, 0), 0)),
            pl.BlockSpec((None, ck, tf), lambda q, i: (layer, w_row(q, i), jnp.minimum(q, nq - 1))),
            pl.BlockSpec((None, ck, tf), lambda q, i: (layer, w_row(q, i), nq + jnp.minimum(q, nq - 1))),
            pl.BlockSpec((None, FFN_CONV, tf), lambda q, i: (layer, 0, prev(q))),
            pl.BlockSpec((None, 1, tf), lambda q, i: (layer, 0, prev(q))),
            pl.BlockSpec((nseq, HIST_ROWS, tf), lambda q, i: (0, 0, prev(q))),
        ],
        out_specs=(
            pl.BlockSpec((tm, tf), lambda q, i: (jnp.where(q >= 1, i, 0), prev(q))),
            pl.BlockSpec((nseq, HIST_ROWS, tf), lambda q, i: (0, 0, prev(q))),
        ),
        scratch_shapes=[pltpu.VMEM((d, 2 * tf), BF16), pltpu.VMEM((d, 2 * tf), BF16),
                        pltpu.VMEM((tm + HIST_ROWS, tf), F32)],
        compiler_params=pltpu.CompilerParams(dimension_semantics=("arbitrary", "arbitrary"), vmem_limit_bytes=VMEM_LIMIT_BIG),
    )(hn, w_up, w_up, conv_w, conv_b.reshape(conv_b.shape[0], 1, D_FF), hist)


def _lru_kernel(*refs, T, nt, aliased):
    (xa_ref, ga_ref, hist_ref, h0_ref, cw_ref, cb_ref, wr_ref, br_ref, wi_ref, bi_ref, lam_ref) = refs[:11]
    rest = refs[12:] if aliased else refs[11:]
    y_ref, cst_ref, hl_ref, buf_ref, hc_ref = rest
    t = pl.program_id(2)
    tc = xa_ref.shape[1]

    @pl.when(t == 0)
    def _():
        hc_ref[...] = h0_ref[0]

    xc = _conv_step(buf_ref, xa_ref[...], hist_ref, cw_ref, cb_ref, LRU_CONV, t, T)
    xcb = xc.astype(BF16)
    rs, gs = [], []
    for hh in range(tc // LRU_BLOCK):
        xb = xcb[:, hh * LRU_BLOCK:(hh + 1) * LRU_BLOCK]
        rs.append(jnp.dot(xb, wr_ref[hh].astype(BF16), preferred_element_type=F32))
        gs.append(jnp.dot(xb, wi_ref[hh].astype(BF16), preferred_element_type=F32))
    r = jax.nn.sigmoid(jnp.concatenate(rs, axis=-1) + br_ref[...])
    ig = jax.nn.sigmoid(jnp.concatenate(gs, axis=-1) + bi_ref[...])
    log_a = -LRU_C * r * _softplus(-lam_ref[...])
    a = jnp.exp(log_a)
    u = jnp.sqrt(-(jnp.tanh(log_a) * (jnp.exp(2.0 * log_a) + 1.0))) * (ig * xc)

    row = lax.broadcasted_iota(jnp.int32, (T, tc), 0)
    s = 1
    while s < T:
        keep = row >= s
        u = u + a * jnp.where(keep, pltpu.roll(u, s, 0), 0.0)
        a = a * jnp.where(keep, pltpu.roll(a, s, 0), 1.0)
        s *= 2
    h = u + a * hc_ref[...]
    hc_ref[...] = h[T - 1:T, :]
    y_ref[...] = (h * _gelu(ga_ref[...])).astype(BF16)

    @pl.when(t == nt - 1)
    def _():
        cst_ref[0] = buf_ref[T:T + HIST_ROWS, :]
        hl_ref[0] = h[T - 1:T, :]


def _lru_call(proj, y_prev, hist, h0, p, e, *, n_seq, L, T, row0, m):
    nt = L // T
    rb0 = row0 // T
    tc = LRU_TILE
    nc = LRU_WIDTH // tc
    hpt = tc // LRU_BLOCK
    row = lambda s, c, t: rb0 + s * nt + t
    vec = lambda: pl.BlockSpec((None, 1, tc), lambda s, c, t: (e, 0, c))
    in_specs = [
        pl.BlockSpec((T, tc), lambda s, c, t: (row(s, c, t), c)),
        pl.BlockSpec((T, tc), lambda s, c, t: (row(s, c, t), nc + c)),
        pl.BlockSpec((1, HIST_ROWS, tc), lambda s, c, t: (s, 0, c)),
        pl.BlockSpec((1, 1, tc), lambda s, c, t: (s, 0, c)),
        pl.BlockSpec((None, LRU_CONV, tc), lambda s, c, t: (e, 0, c)),
        vec(),
        pl.BlockSpec((None, hpt, LRU_BLOCK, LRU_BLOCK), lambda s, c, t: (e, c, 0, 0)),
        vec(),
        pl.BlockSpec((None, hpt, LRU_BLOCK, LRU_BLOCK), lambda s, c, t: (e, c, 0, 0)),
        vec(),
        vec(),
    ]
    args = [proj, proj, hist, h0, p["lru_conv_w"], p["lru_conv_b"], p["lru_w_r"], p["lru_b_r"], p["lru_w_i"], p["lru_b_i"],
            p["lru_lambda"]]
    aliases = {}
    if y_prev is not None:
        in_specs.append(pl.BlockSpec(memory_space=pl.ANY))
        args.append(y_prev)
        aliases = {len(args) - 1: 0}
    return pl.pallas_call(
        functools.partial(_lru_kernel, T=T, nt=nt, aliased=y_prev is not None),
        name=f"lru_L{L}",
        out_shape=(jax.ShapeDtypeStruct((m, D_MODEL), BF16),
                   jax.ShapeDtypeStruct((n_seq, HIST_ROWS, LRU_WIDTH), F32),
                   jax.ShapeDtypeStruct((n_seq, 1, LRU_WIDTH), F32)),
        grid=(n_seq, nc, nt),
        in_specs=in_specs,
        out_specs=(pl.BlockSpec((T, tc), lambda s, c, t: (row(s, c, t), c)),
                   pl.BlockSpec((1, HIST_ROWS, tc), lambda s, c, t: (s, 0, c)),
                   pl.BlockSpec((1, 1, tc), lambda s, c, t: (s, 0, c))),
        scratch_shapes=[pltpu.VMEM((T + HIST_ROWS, tc), F32), pltpu.VMEM((1, tc), F32)],
        input_output_aliases=aliases,
        compiler_params=pltpu.CompilerParams(dimension_semantics=("arbitrary", "arbitrary", "arbitrary"),
                                             vmem_limit_bytes=VMEM_LIMIT_SMALL),
    )(*args)


def _ssd_kernel(*refs, T, nt):
    (z_ref, x_ref, b_ref, c_ref, dt_ref, hx_ref, hb_ref, hcn_ref, s0_ref,
     cwx_ref, cbx_ref, cwb_ref, cbb_ref, cwc_ref, cbc_ref, alog_ref, dd_ref, nw_ref, _y_prev,
     y_ref, csx_ref, csb_ref, csc_ref, sf_ref, bufx_ref, bufb_ref, bufc_ref, s_ref, yd_ref) = refs
    t = pl.program_id(2)
    P = SSD_HEAD_DIM
    gw, n = GROUP_WIDTH, SSD_STATE
    width = x_ref.shape[1]

    xs_all = _silu(_conv_step(bufx_ref, x_ref[...], hx_ref, cwx_ref, cbx_ref, SSD_CONV, t, T))
    bm_all = _silu(_conv_step(bufb_ref, b_ref[...], hb_ref, cwb_ref, cbb_ref, SSD_CONV, t, T))
    cm_all = _silu(_conv_step(bufc_ref, c_ref[...], hcn_ref, cwc_ref, cbc_ref, SSD_CONV, t, T))

    @pl.when(t == 0)
    def _():
        s_ref[...] = s0_ref[0]

    dt = dt_ref[...]
    adt = -jnp.exp(alog_ref[...]) * dt
    row = lax.broadcasted_iota(jnp.int32, (T, width), 0)
    acs_all = adt
    s = 1
    while s < T:
        acs_all = acs_all + jnp.where(row >= s, pltpu.roll(acs_all, s, 0), 0.0)
        s *= 2
    xd_all = xs_all * dt
    eacs_all = jnp.exp(acs_all)
    last_all = acs_all[T - 1:T, :]
    xdd_all = xd_all * jnp.exp(last_all - acs_all)
    elast_all = jnp.exp(last_all)
    tri = lax.broadcasted_iota(jnp.int32, (T, T), 0) >= lax.broadcasted_iota(jnp.int32, (T, T), 1)

    G = SSD_GROUPS_PER_STEP
    HPL = LANES // P
    gcols = [slice(gg * gw, (gg + 1) * gw) for gg in range(G)]
    bmb = [bm_all[:, gg * n:(gg + 1) * n].astype(BF16) for gg in range(G)]
    cmb = [cm_all[:, gg * n:(gg + 1) * n].astype(BF16) for gg in range(G)]
    xdb_all = xd_all.astype(BF16)
    cb = [lax.dot_general(cmb[gg], bmb[gg], (((1,), (1,)), ((), ())), preferred_element_type=F32) for gg in range(G)]
    st = [s_ref[gg] for gg in range(G)]
    y_off = [jnp.dot(cmb[gg], st[gg].astype(BF16), preferred_element_type=F32) for gg in range(G)]
    st_inc = [lax.dot_general(bmb[gg], xdd_all[:, gcols[gg]].astype(BF16), (((0,), (0,)), ((), ())),
                              preferred_element_type=F32) for gg in range(G)]
    blk_t = []
    for b in range(width // LANES):
        blk = acs_all[:, b * LANES:(b + 1) * LANES]
        if T < LANES:
            blk = jnp.concatenate([blk, jnp.zeros((LANES - T, LANES), F32)], axis=0)
        blk_t.append(blk.T)
    md = []
    for h in range(width // P):
        col = acs_all[:, h * P:h * P + 1]
        rowv = blk_t[h // HPL][(h % HPL) * P:(h % HPL) * P + 1, :T]
        decay = jnp.exp(jnp.where(tri, col - rowv, -jnp.inf))
        md.append((cb[h * P // gw] * decay).astype(BF16))
    for h in range(width // P):
        yd_ref[:, h * P:(h + 1) * P] = jnp.dot(md[h], xdb_all[:, h * P:(h + 1) * P], preferred_element_type=F32)
    for gg in range(G):
        cols = gcols[gg]
        s_ref[gg] = st[gg] * elast_all[:, cols] + st_inc[gg]
        yt = yd_ref[:, cols] + y_off[gg] * eacs_all[:, cols] + dd_ref[:, cols] * xs_all[:, cols]
        yt = yt * _silu(z_ref[:, cols])
        yt = yt * lax.rsqrt(jnp.mean(yt * yt, axis=-1, keepdims=True) + NORM_EPS) * nw_ref[:, cols]
        y_ref[:, cols] = yt.astype(BF16)

    @pl.when(t == nt - 1)
    def _():
        csx_ref[0] = bufx_ref[T:T + HIST_ROWS, :]
        csb_ref[0] = bufb_ref[T:T + HIST_ROWS, :]
        csc_ref[0] = bufc_ref[T:T + HIST_ROWS, :]
        sf_ref[0] = s_ref[...]


def _ssd_call(proj, dtr, y_prev, hist, s0, p, e, *, n_seq, L, T, row0, m):
    nt = L // T
    rb0 = row0 // T
    gs = SSD_GROUPS_PER_STEP
    G = SSD_GROUPS
    gw = gs * GROUP_WIDTH
    n = gs * SSD_STATE
    row = lambda s, g, t: rb0 + s * nt + t
    zb = OFF_SSD_Z // gw
    xb = OFF_SSD_XBC // gw
    bb = (OFF_SSD_XBC + SSD_INNER) // n
    cb = (OFF_SSD_XBC + SSD_INNER + SSD_BC) // n
    hb = SSD_INNER // n
    hc = (SSD_INNER + SSD_BC) // n
    vec = lambda: pl.BlockSpec((None, 1, gw), lambda s, g, t: (e, 0, g))
    in_specs = [
        pl.BlockSpec((T, gw), lambda s, g, t: (row(s, g, t), zb + g)),
        pl.BlockSpec((T, gw), lambda s, g, t: (row(s, g, t), xb + g)),
        pl.BlockSpec((T, n), lambda s, g, t: (row(s, g, t), bb + g)),
        pl.BlockSpec((T, n), lambda s, g, t: (row(s, g, t), cb + g)),
        pl.BlockSpec((T, gw), lambda s, g, t: (row(s, g, t), g)),
        pl.BlockSpec((1, HIST_ROWS, gw), lambda s, g, t: (s, 0, g)),
        pl.BlockSpec((1, HIST_ROWS, n), lambda s, g, t: (s, 0, hb + g)),
        pl.BlockSpec((1, HIST_ROWS, n), lambda s, g, t: (s, 0, hc + g)),
        pl.BlockSpec((1, gs, SSD_STATE, GROUP_WIDTH), lambda s, g, t: (s, g, 0, 0)),
        pl.BlockSpec((None, SSD_CONV, gw), lambda s, g, t: (e, 0, g)),
        pl.BlockSpec((None, 1, gw), lambda s, g, t: (e, 0, g)),
        pl.BlockSpec((None, SSD_CONV, n), lambda s, g, t: (e, 0, hb + g)),
        pl.BlockSpec((None, 1, n), lambda s, g, t: (e, 0, hb + g)),
        pl.BlockSpec((None, SSD_CONV, n), lambda s, g, t: (e, 0, hc + g)),
        pl.BlockSpec((None, 1, n), lambda s, g, t: (e, 0, hc + g)),
        vec(), vec(), vec(),
        pl.BlockSpec(memory_space=pl.ANY),
    ]
    args = [proj, proj, proj, proj, dtr, hist, hist, hist, s0,
            p["ssd_conv_w"], p["ssd_conv_b"], p["ssd_conv_w"], p["ssd_conv_b"], p["ssd_conv_w"], p["ssd_conv_b"],
            p["ssd_a_log_x"], p["ssd_d_x"], p["ssd_norm_w"], y_prev]
    yb = LRU_WIDTH // gw
    return pl.pallas_call(
        functools.partial(_ssd_kernel, T=T, nt=nt),
        name=f"ssd_L{L}",
        out_shape=(jax.ShapeDtypeStruct((m, D_MODEL), BF16),
                   jax.ShapeDtypeStruct((n_seq, HIST_ROWS, SSD_INNER), F32),
                   jax.ShapeDtypeStruct((n_seq, HIST_ROWS, SSD_BC), F32),
                   jax.ShapeDtypeStruct((n_seq, HIST_ROWS, SSD_BC), F32),
                   jax.ShapeDtypeStruct((n_seq, G, SSD_STATE, GROUP_WIDTH), F32)),
        grid=(n_seq, G // gs, nt),
        in_specs=in_specs,
        out_specs=(pl.BlockSpec((T, gw), lambda s, g, t: (row(s, g, t), yb + g)),
                   pl.BlockSpec((1, HIST_ROWS, gw), lambda s, g, t: (s, 0, g)),
                   pl.BlockSpec((1, HIST_ROWS, n), lambda s, g, t: (s, 0, g)),
                   pl.BlockSpec((1, HIST_ROWS, n), lambda s, g, t: (s, 0, g)),
                   pl.BlockSpec((1, gs, SSD_STATE, GROUP_WIDTH), lambda s, g, t: (s, g, 0, 0))),
        scratch_shapes=[pltpu.VMEM((T + HIST_ROWS, gw), F32), pltpu.VMEM((T + HIST_ROWS, n), F32),
                        pltpu.VMEM((T + HIST_ROWS, n), F32), pltpu.VMEM((gs, SSD_STATE, GROUP_WIDTH), F32),
                        pltpu.VMEM((T, gw), F32)],
        input_output_aliases={len(args) - 1: 0},
        compiler_params=pltpu.CompilerParams(dimension_semantics=("arbitrary", "arbitrary", "arbitrary"),
                                             vmem_limit_bytes=VMEM_LIMIT_SMALL),
    )(*args)


def _ret_kernel(*refs, nt, n_aliased):
    (q_ref, k_ref, v_ref, g_ref, cos_ref, sin_ref, dm_ref, cr_ref, kd_ref, cd_ref, s0_ref) = refs[:11]
    y_ref, sf_ref, s_ref = refs[11 + n_aliased:]
    t = pl.program_id(2)
    half = RET_QK_DIM // 2
    dk, dv = RET_QK_DIM, RET_V_DIM

    @pl.when(t == 0)
    def _():
        s_ref[...] = s0_ref[0]

    cos = cos_ref[...]
    sin = sin_ref[...]

    def rot(x):
        x1 = x[:, :half]
        x2 = x[:, half:]
        return jnp.concatenate([x1 * cos - x2 * sin, x2 * cos + x1 * sin], axis=-1)

    for hh in range(RET_HEADS_PER_STEP):
        q = rot(q_ref[:, hh * dk:(hh + 1) * dk].astype(F32))
        k = rot(k_ref[:, hh * dk:(hh + 1) * dk].astype(F32)) * (RET_QK_DIM ** -0.5)
        qb = q.astype(BF16)
        vb = v_ref[:, hh * dv:(hh + 1) * dv].astype(BF16)
        att = lax.dot_general(qb, k.astype(BF16), (((1,), (1,)), ((), ())), preferred_element_type=F32) * dm_ref[hh]
        st = s_ref[hh]
        o = (jnp.dot(att.astype(BF16), vb, preferred_element_type=F32)
             + jnp.dot((q * cr_ref[hh]).astype(BF16), st.astype(BF16), preferred_element_type=F32))
        st_new = st * cd_ref[hh][0:1, :] + lax.dot_general((k * kd_ref[hh]).astype(BF16), vb, (((0,), (0,)), ((), ())),
                                                            preferred_element_type=F32)
        s_ref[hh] = st_new
        o = o * lax.rsqrt(jnp.mean(o * o, axis=-1, keepdims=True) + NORM_EPS)
        y_ref[:, hh * dv:(hh + 1) * dv] = (_silu(g_ref[:, hh * dv:(hh + 1) * dv].astype(F32)) * o).astype(BF16)

    @pl.when(t == nt - 1)
    def _():
        sf_ref[0] = s_ref[...]


def _ret_tables(T, ct, pos0, L):
    f32 = F32
    log_gamma = jnp.log1p(-jnp.exp2(-5.0 - jnp.arange(RET_HEADS, dtype=f32)))
    n = jnp.arange(T, dtype=f32)
    chunk = jnp.arange(T) // ct
    same = chunk[:, None] == chunk[None, :]
    earlier = chunk[None, :] < chunk[:, None]
    diff = n[:, None] - n[None, :]
    lg = log_gamma[:, None, None]
    dm = jnp.where(same[None], jnp.exp(jnp.abs(diff)[None] * lg), jnp.where(earlier[None], jnp.exp(diff[None] * lg), 0.0))
    cross = jnp.exp((n[None, :] + 1.0) * log_gamma[:, None])
    kvd = jnp.exp((T - 1.0 - n)[None, :] * log_gamma[:, None])
    cdec = jnp.exp(T * log_gamma)
    cr = jnp.broadcast_to(cross[:, :, None], (RET_HEADS, T, RET_QK_DIM))
    kd = jnp.broadcast_to(kvd[:, :, None], (RET_HEADS, T, RET_QK_DIM))
    cd = jnp.broadcast_to(cdec[:, None, None], (RET_HEADS, SUBLANES, RET_V_DIM))
    half = RET_QK_DIM // 2
    inv = 1.0 / (ROPE_BASE ** jnp.linspace(0.0, 1.0, half, dtype=f32))
    pos = float(pos0) + jnp.arange(L, dtype=f32)
    ang = pos[:, None] * inv[None, :]
    return dm, cr, kd, cd, jnp.cos(ang), jnp.sin(ang)


def _ret_call(proj, y_prev, s0, e0, sf_prev, e, n_layers, *, n_seq, L, T, row0, pos0, m):
    nt = L // T
    rb0 = row0 // T
    ct = min(CHUNK, L)
    assert T % ct == 0
    dm, cr, kd, cd, cos, sin = _ret_tables(T, ct, pos0, L)
    H = RET_HEADS
    hs = RET_HEADS_PER_STEP
    dk, dv = hs * RET_QK_DIM, hs * RET_V_DIM
    row = lambda s, h, t: rb0 + s * nt + t
    kb0 = RET_QK_WIDTH // dk
    vb0 = 2 * RET_QK_WIDTH // dv
    gb0 = (2 * RET_QK_WIDTH + RET_V_WIDTH) // dv
    in_specs = [
        pl.BlockSpec((T, dk), lambda s, h, t: (row(s, h, t), h)),
        pl.BlockSpec((T, dk), lambda s, h, t: (row(s, h, t), kb0 + h)),
        pl.BlockSpec((T, dv), lambda s, h, t: (row(s, h, t), vb0 + h)),
        pl.BlockSpec((T, dv), lambda s, h, t: (row(s, h, t), gb0 + h)),
        pl.BlockSpec((T, RET_QK_DIM // 2), lambda s, h, t: (t, 0)),
        pl.BlockSpec((T, RET_QK_DIM // 2), lambda s, h, t: (t, 0)),
        pl.BlockSpec((hs, T, T), lambda s, h, t: (h, 0, 0)),
        pl.BlockSpec((hs, T, RET_QK_DIM), lambda s, h, t: (h, 0, 0)),
        pl.BlockSpec((hs, T, RET_QK_DIM), lambda s, h, t: (h, 0, 0)),
        pl.BlockSpec((hs, SUBLANES, RET_V_DIM), lambda s, h, t: (h, 0, 0)),
        pl.BlockSpec((None, 1, hs, RET_QK_DIM, RET_V_DIM), lambda s, h, t: (e0, s, h, 0, 0)),
    ]
    args = [proj, proj, proj, proj, cos, sin, dm, cr, kd, cd, s0]
    aliases = {}
    for out_idx, prev in ((0, y_prev), (1, sf_prev)):
        if prev is not None:
            in_specs.append(pl.BlockSpec(memory_space=pl.ANY))
            args.append(prev)
            aliases[len(args) - 1] = out_idx
    return pl.pallas_call(
        functools.partial(_ret_kernel, nt=nt, n_aliased=len(aliases)),
        name=f"ret_L{L}",
        out_shape=(jax.ShapeDtypeStruct((m, RET_V_WIDTH), BF16),
                   jax.ShapeDtypeStruct((n_layers, n_seq, H, RET_QK_DIM, RET_V_DIM), F32)),
        grid=(n_seq, H // hs, nt),
        in_specs=in_specs,
        out_specs=(pl.BlockSpec((T, dv), lambda s, h, t: (row(s, h, t), h)),
                   pl.BlockSpec((None, 1, hs, RET_QK_DIM, RET_V_DIM), lambda s, h, t: (e, s, h, 0, 0))),
        scratch_shapes=[pltpu.VMEM((hs, RET_QK_DIM, RET_V_DIM), F32)],
        input_output_aliases=aliases,
        compiler_params=pltpu.CompilerParams(dimension_semantics=("arbitrary", "arbitrary", "arbitrary"),
                                             vmem_limit_bytes=VMEM_LIMIT_SMALL),
    )(*args)


def _pad_hist(state):
    return jnp.pad(state, ((0, 0), (HIST_ROWS - state.shape[1], 0), (0, 0)))


def _ssd_state_in(s):
    n = s.shape[0]
    s = s.reshape(n, SSD_GROUPS, HEADS_PER_GROUP, SSD_HEAD_DIM, SSD_STATE)
    return s.transpose(0, 1, 4, 2, 3).reshape(n, SSD_GROUPS, SSD_STATE, GROUP_WIDTH)


def _ssd_state_out(s):
    n = s.shape[0]
    s = s.reshape(n, SSD_GROUPS, SSD_STATE, HEADS_PER_GROUP, SSD_HEAD_DIM)
    return s.transpose(0, 1, 3, 4, 2).reshape(n, SSD_HEADS, SSD_HEAD_DIM, SSD_STATE)


def kernel(x_prompt, x_sample, state_lru_conv, state_lru_h, state_ssd_conv, state_ssd, state_ret, state_ffn_conv,
           norm_mix_pre, norm_mix_post, norm_ffn_pre, norm_ffn_post, w_in_ab, lru_conv_w, lru_conv_b, lru_w_r, lru_b_r,
           lru_w_i, lru_b_i, lru_lambda, ssd_conv_w, ssd_conv_b, ssd_dt_bias, ssd_a_log, ssd_d, ssd_norm_w, w_out_ab,
           w_in_ret, w_out_ret, ffn_w_up, ffn_conv_w, ffn_conv_b, ffn_w_down):
    bp, lp, d = x_prompt.shape
    bs, ls, _ = x_sample.shape
    depth = norm_mix_pre.shape[0]
    n_even = w_in_ab.shape[0]
    mp = bp * lp
    m = mp + bs * ls

    groups = ((bp, lp, 0, 0), (bs, ls, mp, PAST_LEN))
    t_lru = (min(256, lp), ls)
    t_ssd = (min(128, lp), ls)
    t_ret = (min(256, lp), ls)

    rep = lambda v: jnp.repeat(v, SSD_HEAD_DIM, axis=-1).reshape(n_even, 1, SSD_INNER)
    vec3 = lambda v: v.reshape(v.shape[0], 1, v.shape[-1])
    ab = {
        "lru_conv_w": lru_conv_w, "lru_conv_b": vec3(lru_conv_b), "lru_w_r": lru_w_r, "lru_b_r": vec3(lru_b_r),
        "lru_w_i": lru_w_i, "lru_b_i": vec3(lru_b_i), "lru_lambda": vec3(lru_lambda),
        "ssd_conv_w": ssd_conv_w, "ssd_conv_b": vec3(ssd_conv_b),
        "ssd_a_log_x": rep(ssd_a_log), "ssd_d_x": rep(ssd_d),
        "ssd_norm_w": vec3(ssd_norm_w),
    }
    ffn_hist = _pad_hist(state_ffn_conv.reshape(depth * bs, FFN_CONV - 1, D_FF)).reshape(depth, bs, HIST_ROWS, D_FF)
    ffn_hist = jnp.pad(ffn_hist, ((0, 0), (bp, 0), (0, 0), (0, 0)))

    new = {k: ([], []) for k in ("lc", "lh", "sc", "ss", "fc")}
    ret_stack = [None, None]
    w_in_ab_t = jnp.swapaxes(w_in_ab, 1, 2)

    x, hn = _first_norm(x_prompt.reshape(mp, d), x_sample.reshape(bs * ls, d), norm_mix_pre[0])
    for layer in range(depth):
        e = layer // 2
        if layer % 2 == 0:
            proj = _matmul(hn, w_in_ab_t, e, n_cols=OFF_SSD_DT, w_transposed=True)
            dtr = _dt_proj(hn, w_in_ab_t, ssd_dt_bias, e)
            y = None
            for gi, (n_seq, L, row0, _) in enumerate(groups):
                if gi == 0:
                    hist = jnp.zeros((n_seq, HIST_ROWS, LRU_WIDTH), F32)
                    h0 = jnp.zeros((n_seq, 1, LRU_WIDTH), F32)
                else:
                    hist = _pad_hist(state_lru_conv[e])
                    h0 = state_lru_h[e].reshape(n_seq, 1, LRU_WIDTH)
                y, cst, hl = _lru_call(proj, y, hist, h0, ab, e, n_seq=n_seq, L=L, T=t_lru[gi], row0=row0, m=m)
                new["lc"][gi].append(cst[:, HIST_ROWS - (LRU_CONV - 1):, :])
                new["lh"][gi].append(hl.reshape(n_seq, LRU_WIDTH))
            for gi, (n_seq, L, row0, _) in enumerate(groups):
                if gi == 0:
                    hist = jnp.zeros((n_seq, HIST_ROWS, SSD_CONV_DIM), F32)
                    s0 = jnp.zeros((n_seq, SSD_GROUPS, SSD_STATE, GROUP_WIDTH), F32)
                else:
                    hist = _pad_hist(state_ssd_conv[e])
                    s0 = _ssd_state_in(state_ssd[e])
                y, csx, csb, csc, sf = _ssd_call(proj, dtr, y, hist, s0, ab, e, n_seq=n_seq, L=L, T=t_ssd[gi], row0=row0, m=m)
                cst = jnp.concatenate([csx, csb, csc], axis=-1)
                new["sc"][gi].append(cst[:, HIST_ROWS - (SSD_CONV - 1):, :])
                new["ss"][gi].append(_ssd_state_out(sf))
            mix = _matmul(y, w_out_ab, e, out_dtype=BF16)
        else:
            proj = _matmul(hn, w_in_ret, e, out_dtype=BF16)
            y = None
            for gi, (n_seq, L, row0, pos0) in enumerate(groups):
                if gi == 0:
                    s0, e0 = jnp.zeros((1, n_seq, RET_HEADS, RET_QK_DIM, RET_V_DIM), F32), 0
                else:
                    s0, e0 = state_ret, e
                y, ret_stack[gi] = _ret_call(proj, y, s0, e0, ret_stack[gi], e, depth // 2, n_seq=n_seq, L=L, T=t_ret[gi],
                                             row0=row0, pos0=pos0, m=m)
            mix = _matmul(y, w_out_ret, e, out_dtype=BF16)
        x, hn = _row_call(_resid_norm_kernel, [x, mix], [norm_mix_post[layer], norm_ffn_pre[layer]], [F32, BF16])
        h, fst = _ffn_up(hn, ffn_w_up, ffn_conv_w, ffn_conv_b, ffn_hist[layer], layer, bp=bp, lp=lp, bs=bs, ls=ls)
        fst = fst[:, HIST_ROWS - (FFN_CONV - 1):, :]
        new["fc"][0].append(fst[:bp])
        new["fc"][1].append(fst[bp:])
        f = _matmul(h, ffn_w_down, layer, out_dtype=BF16)
        if layer + 1 < depth:
            x, hn = _row_call(_resid_norm_kernel, [x, f], [norm_ffn_post[layer], norm_mix_pre[layer + 1]], [F32, BF16])
        else:
            y_prompt, y_sample = _last_resid(x, f, norm_ffn_post[layer], mp)

    outs = [y_prompt.reshape(bp, lp, d), y_sample.reshape(bs, ls, d)]
    for gi in range(2):
        for k in ("lc", "lh", "sc", "ss", "rs", "fc"):
            outs.append(ret_stack[gi] if k == "rs" else jnp.stack(new[k][gi]))
    return tuple(outs)
```

```python
import functools

import jax
import jax.numpy as jnp
from jax import lax
from jax.experimental import pallas as pl
from jax.experimental.pallas import tpu as pltpu

F32 = jnp.float32
BF16 = jnp.bfloat16

D_MODEL = 4096
PAST_LEN = 1024
CHUNK = 64
NORM_EPS = 1e-6
LRU_WIDTH = D_MODEL // 2
LRU_HEADS = 16
LRU_BLOCK = LRU_WIDTH // LRU_HEADS
LRU_CONV = 4
LRU_C = 8.0
SSD_INNER = D_MODEL // 2
SSD_HEAD_DIM = 64
SSD_HEADS = SSD_INNER // SSD_HEAD_DIM
SSD_GROUPS = 8
SSD_STATE = 128
SSD_CONV = 4
SSD_BC = SSD_GROUPS * SSD_STATE
SSD_CONV_DIM = SSD_INNER + 2 * SSD_BC
OFF_SSD_Z = 2 * LRU_WIDTH
OFF_SSD_XBC = OFF_SSD_Z + SSD_INNER
OFF_SSD_DT = OFF_SSD_XBC + SSD_CONV_DIM
RET_HEADS = 16
RET_QK_DIM = D_MODEL // RET_HEADS
RET_V_DIM = 2 * RET_QK_DIM
RET_QK_WIDTH = RET_HEADS * RET_QK_DIM
RET_V_WIDTH = RET_HEADS * RET_V_DIM
ROPE_BASE = 10000.0
D_FF = 3 * D_MODEL
FFN_CONV = 3

SUBLANES = 8
LANES = 128
BF16_ROWS = 16
HIST_ROWS = SUBLANES
MIB = 1 << 20
VMEM_LIMIT_BIG = 56 * MIB
VMEM_LIMIT_SMALL = 40 * MIB

HEADS_PER_GROUP = SSD_HEADS // SSD_GROUPS
GROUP_WIDTH = HEADS_PER_GROUP * SSD_HEAD_DIM
LRU_TILE = 512
MATMUL_TM_CAP = 1056
MATMUL_TK = 4096
MATMUL_TN = 1024
FFN_TILE = 1024
FFN_COLS = 512
NORM_ROWS = 256
RET_HEADS_PER_STEP = 4
SSD_GROUPS_PER_STEP = 4


def _pick_tile(total, cap, mult):
    best = None
    for t in range(mult, min(total, cap) + 1, mult):
        if total % t == 0:
            best = t
    assert best is not None, (total, cap, mult)
    return best


def _silu(x):
    return x * jax.nn.sigmoid(x)


def _gelu(x):
    return jax.nn.gelu(x, approximate=True)


def _softplus(x):
    return jnp.maximum(x, 0.0) + jnp.log1p(jnp.exp(-jnp.abs(x)))


def _rms(x, w):
    return x * lax.rsqrt(jnp.mean(x * x, axis=-1, keepdims=True) + NORM_EPS) * w


def _emit_dt(hn, wdt_ref, bdt_ref, dt_ref):
    rows = hn.shape[0]
    dt = lax.dot_general(hn, wdt_ref[...].astype(BF16), (((1,), (1,)), ((), ())), preferred_element_type=F32)
    dt = _softplus(dt + bdt_ref[...])
    for h in range(SSD_HEADS):
        dt_ref[:, h * SSD_HEAD_DIM:(h + 1) * SSD_HEAD_DIM] = jnp.broadcast_to(dt[:, h:h + 1], (rows, SSD_HEAD_DIM))


def _resid_norm_kernel(*refs, with_dt):
    if with_dt:
        x_ref, m_ref, wp_ref, wn_ref, wdt_ref, bdt_ref, xo_ref, hn_ref, dt_ref = refs
    else:
        x_ref, m_ref, wp_ref, wn_ref, xo_ref, hn_ref = refs
    xn = x_ref[...] + _rms(m_ref[...].astype(F32), wp_ref[...])
    xo_ref[...] = xn
    hn = _rms(xn, wn_ref[...]).astype(BF16)
    hn_ref[...] = hn
    if with_dt:
        _emit_dt(hn, wdt_ref, bdt_ref, dt_ref)


def _first_norm_kernel(xp_ref, xs_ref, w_ref, wdt_ref, bdt_ref, xo_ref, hn_ref, dt_ref, *, n_ptiles):
    i = pl.program_id(0)

    def emit(x):
        xo_ref[...] = x
        hn = _rms(x, w_ref[...]).astype(BF16)
        hn_ref[...] = hn
        _emit_dt(hn, wdt_ref, bdt_ref, dt_ref)

    @pl.when(i < n_ptiles)
    def _():
        emit(xp_ref[...])

    @pl.when(i >= n_ptiles)
    def _():
        emit(xs_ref[...])


def _last_resid_kernel(x_ref, m_ref, wp_ref, yp_ref, ys_ref, *, n_ptiles):
    i = pl.program_id(0)
    xn = x_ref[...] + _rms(m_ref[...].astype(F32), wp_ref[...])

    @pl.when(i < n_ptiles)
    def _():
        yp_ref[...] = xn

    @pl.when(i >= n_ptiles)
    def _():
        ys_ref[...] = xn


def _row_tile(mp, ms):
    tr = _pick_tile(mp, NORM_ROWS, BF16_ROWS)
    assert ms % tr == 0
    return tr


def _dt_specs(w_t, dt_bias, layer, tr):
    d = w_t.shape[-1]
    assert OFF_SSD_DT % SSD_HEADS == 0
    in_specs = [pl.BlockSpec((None, SSD_HEADS, d), lambda i: (layer, OFF_SSD_DT // SSD_HEADS, 0)),
                pl.BlockSpec((None, 1, SSD_HEADS), lambda i: (layer, 0, 0))]
    args = [w_t, dt_bias.reshape(dt_bias.shape[0], 1, SSD_HEADS)]
    return in_specs, args, pl.BlockSpec((tr, SSD_INNER), lambda i: (i, 0))


def _first_norm(xp, xs, w, w_t, dt_bias):
    mp, d = xp.shape
    ms = xs.shape[0]
    tr = _row_tile(mp, ms)
    n_ptiles = mp // tr
    row_spec = pl.BlockSpec((tr, d), lambda i: (i, 0))
    dt_in_specs, dt_args, dt_out_spec = _dt_specs(w_t, dt_bias, 0, tr)
    return pl.pallas_call(
        functools.partial(_first_norm_kernel, n_ptiles=n_ptiles),
        name="first_norm",
        out_shape=(jax.ShapeDtypeStruct((mp + ms, d), F32), jax.ShapeDtypeStruct((mp + ms, d), BF16),
                   jax.ShapeDtypeStruct((mp + ms, SSD_INNER), F32)),
        grid=((mp + ms) // tr,),
        in_specs=[pl.BlockSpec((tr, d), lambda i: (jnp.minimum(i, n_ptiles - 1), 0)),
                  pl.BlockSpec((tr, d), lambda i: (jnp.maximum(i - n_ptiles, 0), 0)),
                  pl.BlockSpec((1, d), lambda i: (0, 0))] + dt_in_specs,
        out_specs=(row_spec, row_spec, dt_out_spec),
        compiler_params=pltpu.CompilerParams(dimension_semantics=("arbitrary",), vmem_limit_bytes=VMEM_LIMIT_SMALL),
    )(xp, xs, w.reshape(1, d), *dt_args)


def _last_resid(x, mix, w, mp):
    m, d = x.shape
    ms = m - mp
    tr = _row_tile(mp, ms)
    n_ptiles = mp // tr
    row_spec = pl.BlockSpec((tr, d), lambda i: (i, 0))
    return pl.pallas_call(
        functools.partial(_last_resid_kernel, n_ptiles=n_ptiles),
        name="last_resid",
        out_shape=(jax.ShapeDtypeStruct((mp, d), F32), jax.ShapeDtypeStruct((ms, d), F32)),
        grid=(m // tr,),
        in_specs=[row_spec, row_spec, pl.BlockSpec((1, d), lambda i: (0, 0))],
        out_specs=(pl.BlockSpec((tr, d), lambda i: (jnp.minimum(i, n_ptiles - 1), 0)),
                   pl.BlockSpec((tr, d), lambda i: (jnp.maximum(i - n_ptiles, 0), 0))),
        compiler_params=pltpu.CompilerParams(dimension_semantics=("arbitrary",), vmem_limit_bytes=VMEM_LIMIT_SMALL),
    )(x, mix, w.reshape(1, d))


def _resid_norm(x, mix, w_post, w_next, dt_of=None):
    m, d = x.shape
    tr = _pick_tile(m, NORM_ROWS, BF16_ROWS)
    row_spec = pl.BlockSpec((tr, d), lambda i: (i, 0))
    w_spec = pl.BlockSpec((1, d), lambda i: (0, 0))
    in_specs = [row_spec, row_spec, w_spec, w_spec]
    args = [x, mix, w_post.reshape(1, d), w_next.reshape(1, d)]
    out_shape = [jax.ShapeDtypeStruct((m, d), F32), jax.ShapeDtypeStruct((m, d), BF16)]
    out_specs = [row_spec, row_spec]
    if dt_of is not None:
        dt_in_specs, dt_args, dt_out_spec = _dt_specs(*dt_of, tr)
        in_specs += dt_in_specs
        args += dt_args
        out_shape.append(jax.ShapeDtypeStruct((m, SSD_INNER), F32))
        out_specs.append(dt_out_spec)
    return pl.pallas_call(
        functools.partial(_resid_norm_kernel, with_dt=dt_of is not None),
        name="resid_norm",
        out_shape=tuple(out_shape),
        grid=(m // tr,),
        in_specs=in_specs,
        out_specs=tuple(out_specs),
        compiler_params=pltpu.CompilerParams(dimension_semantics=("arbitrary",), vmem_limit_bytes=VMEM_LIMIT_SMALL),
    )(*args)


def _matmul_kernel(a_ref, w_ref, o_ref, wb0_ref, wb1_ref, *, nq, ck, w_transposed):
    q = pl.program_id(0)
    i = pl.program_id(1)
    wb_refs = (wb0_ref, wb1_ref)

    def stage(par):
        w = w_ref[...]
        if w_transposed:
            w = w.T
        wb_refs[par][pl.ds(pl.multiple_of(i * ck, BF16_ROWS), ck), :] = w.astype(BF16)

    def product(par):
        o_ref[...] = jnp.dot(a_ref[...], wb_refs[par][...], preferred_element_type=F32).astype(o_ref.dtype)

    @pl.when(q == 0)
    def _():
        stage(0)

    for par in (0, 1):
        @pl.when(jnp.logical_and(jnp.logical_and(q >= 1, q < nq), q % 2 == par))
        def _():
            stage(par)
            product(1 - par)

    @pl.when(q == nq)
    def _():
        product((nq - 1) % 2)


def _matmul(a, w, layer, *, n_cols=None, out_dtype=F32, w_transposed=False):
    m, k = a.shape
    n = n_cols if n_cols is not None else w.shape[-2 if w_transposed else -1]
    depth_ratio = max(1, k // MATMUL_TK)
    tm = _pick_tile(m, MATMUL_TM_CAP // depth_ratio, BF16_ROWS)
    tn = MATMUL_TN if depth_ratio == 1 else MATMUL_TN // 2
    n_mt = m // tm
    ck = k // n_mt
    assert n % tn == 0 and k % n_mt == 0 and ck % LANES == 0
    nq = n // tn

    def a_map(q, i):
        return (jnp.where(q >= 1, i, 0), 0)

    def w_map(q, i):
        row = jnp.where(q < nq, i, n_mt - 1)
        p = jnp.minimum(q, nq - 1)
        return (layer, p, row) if w_transposed else (layer, row, p)

    def o_map(q, i):
        return (jnp.where(q >= 1, i, 0), jnp.maximum(q - 1, 0))

    return pl.pallas_call(
        functools.partial(_matmul_kernel, nq=nq, ck=ck, w_transposed=w_transposed),
        name=f"matmul_k{k}_n{n}",
        out_shape=jax.ShapeDtypeStruct((m, n), out_dtype),
        grid=(nq + 1, n_mt),
        in_specs=[
            pl.BlockSpec((tm, k), a_map),
            pl.BlockSpec((None, tn, ck) if w_transposed else (None, ck, tn), w_map),
        ],
        out_specs=pl.BlockSpec((tm, tn), o_map),
        scratch_shapes=[pltpu.VMEM((k, tn), BF16), pltpu.VMEM((k, tn), BF16)],
        compiler_params=pltpu.CompilerParams(
            dimension_semantics=("arbitrary", "arbitrary"), vmem_limit_bytes=VMEM_LIMIT_BIG),
    )(a, w)


def _conv_from_buf(buf_ref, cw_ref, cb_ref, width, rows):
    v = buf_ref[0:HIST_ROWS + rows, :]

    def tap(k):
        back = width - 1 - k
        return v[HIST_ROWS:, :] if back == 0 else pltpu.roll(v, back, 0)[HIST_ROWS:, :]

    y = cb_ref[...] + tap(0) * cw_ref[0:1, :]
    for k in range(1, width):
        y = y + tap(k) * cw_ref[k:k + 1, :]
    return y


def _conv_step(buf_ref, src, hist_ref, cw_ref, cb_ref, width, t, rows):
    @pl.when(t == 0)
    def _():
        buf_ref[0:HIST_ROWS, :] = hist_ref[0]

    @pl.when(t > 0)
    def _():
        buf_ref[0:HIST_ROWS, :] = buf_ref[rows:rows + HIST_ROWS, :]

    buf_ref[HIST_ROWS:HIST_ROWS + rows, :] = src
    return _conv_from_buf(buf_ref, cw_ref, cb_ref, width, rows)


def _ffn_up_kernel(a_ref, wg_ref, wu_ref, cw_ref, cb_ref, hist_ref, h_ref, st_ref, wb0_ref, wb1_ref, buf_ref, *,
                   tiles_per_seq, n_ptiles, n_pseq, n_sseq, ls, tf, nq):
    q = pl.program_id(0)
    i = pl.program_id(1)
    tm = a_ref.shape[0]
    ck = wg_ref.shape[0]
    wb_refs = (wb0_ref, wb1_ref)
    H = HIST_ROWS

    def stage(par):
        rows = pl.ds(pl.multiple_of(i * ck, BF16_ROWS), ck)
        wb_refs[par][rows, 0:tf] = wg_ref[...].astype(BF16)
        wb_refs[par][rows, tf:2 * tf] = wu_ref[...].astype(BF16)

    def prompt_tile(par):
        p = jnp.dot(a_ref[...], wb_refs[par][...], preferred_element_type=F32)
        seq = i // tiles_per_seq
        first = (i % tiles_per_seq) == 0
        buf_ref[0:H, :] = jnp.where(first, hist_ref[seq], buf_ref[tm:tm + H, :])
        buf_ref[H:H + tm, :] = p[:, 0:tf]
        gate = _gelu(_conv_from_buf(buf_ref, cw_ref, cb_ref, FFN_CONV, tm))
        h_ref[...] = (gate * p[:, tf:2 * tf]).astype(BF16)
        st_ref[seq] = buf_ref[tm:tm + H, :]

    def sample_tile(par):
        rows = n_sseq * ls
        p = jnp.dot(a_ref[0:rows, :], wb_refs[par][...], preferred_element_type=F32)
        for s in range(n_sseq):
            lo = s * ls
            buf_ref[0:H, :] = hist_ref[n_pseq + s]
            buf_ref[H:H + ls, :] = p[lo:lo + ls, 0:tf]
            gate = _gelu(_conv_from_buf(buf_ref, cw_ref, cb_ref, FFN_CONV, ls))
            h_ref[lo:lo + ls, :] = (gate * p[lo:lo + ls, tf:2 * tf]).astype(BF16)
            st_ref[n_pseq + s] = buf_ref[ls:ls + H, :]

    is_prompt = i < n_ptiles

    @pl.when(jnp.logical_and(q == 0, i == 0))
    def _():
        buf_ref[tm:tm + H, :] = jnp.zeros((H, tf), F32)

    @pl.when(jnp.logical_and(q == 0, is_prompt))
    def _():
        stage(0)

    for par in (0, 1):
        @pl.when(jnp.logical_and(jnp.logical_and(q >= 1, q < nq), jnp.logical_and(q % 2 == par, is_prompt)))
        def _():
            stage(par)
            prompt_tile(1 - par)

        @pl.when(jnp.logical_and(jnp.logical_and(q >= 1, (q - 1) % 2 == par), i == n_ptiles))
        def _():
            sample_tile(par)

    @pl.when(jnp.logical_and(q == nq, is_prompt))
    def _():
        prompt_tile((nq - 1) % 2)


def _ffn_up(hn, w_up, conv_w, conv_b, hist, layer, *, bp, lp, bs, ls):
    m, d = hn.shape
    tf = FFN_COLS
    tm = min(FFN_TILE, lp)
    n_ptiles = bp * lp // tm
    ck = d // n_ptiles
    assert lp % tm == 0 and bs * ls <= tm and ls % BF16_ROWS == 0 and d % n_ptiles == 0 and ck % BF16_ROWS == 0
    nseq = bp + bs
    nq = D_FF // tf
    kern = functools.partial(_ffn_up_kernel, tiles_per_seq=lp // tm, n_ptiles=n_ptiles, n_pseq=bp, n_sseq=bs, ls=ls,
                             tf=tf, nq=nq)
    prev = lambda q: jnp.maximum(q - 1, 0)

    def w_row(q, i):
        return jnp.where(q < nq, jnp.minimum(i, n_ptiles - 1), n_ptiles - 1)

    return pl.pallas_call(
        kern,
        name="ffn_up",
        out_shape=(jax.ShapeDtypeStruct((m, D_FF), BF16), jax.ShapeDtypeStruct((nseq, HIST_ROWS, D_FF), F32)),
        grid=(nq + 1, n_ptiles + 1),
        in_specs=[
            pl.BlockSpec((tm, d), lambda q, i: (jnp.where(q >= 1, i, 0), 0)),
            pl.BlockSpec((None, ck, tf), lambda q, i: (layer, w_row(q, i), jnp.minimum(q, nq - 1))),
            pl.BlockSpec((None, ck, tf), lambda q, i: (layer, w_row(q, i), nq + jnp.minimum(q, nq - 1))),
            pl.BlockSpec((None, FFN_CONV, tf), lambda q, i: (layer, 0, prev(q))),
            pl.BlockSpec((None, 1, tf), lambda q, i: (layer, 0, prev(q))),
            pl.BlockSpec((nseq, HIST_ROWS, tf), lambda q, i: (0, 0, prev(q))),
        ],
        out_specs=(
            pl.BlockSpec((tm, tf), lambda q, i: (jnp.where(q >= 1, i, 0), prev(q))),
            pl.BlockSpec((nseq, HIST_ROWS, tf), lambda q, i: (0, 0, prev(q))),
        ),
        scratch_shapes=[pltpu.VMEM((d, 2 * tf), BF16), pltpu.VMEM((d, 2 * tf), BF16),
                        pltpu.VMEM((tm + HIST_ROWS, tf), F32)],
        compiler_params=pltpu.CompilerParams(dimension_semantics=("arbitrary", "arbitrary"), vmem_limit_bytes=VMEM_LIMIT_BIG),
    )(hn, w_up, w_up, conv_w, conv_b.reshape(conv_b.shape[0], 1, D_FF), hist)


def _lru_kernel(*refs, T, nt, aliased):
    (xa_ref, ga_ref, hist_ref, h0_ref, cw_ref, cb_ref, wr_ref, br_ref, wi_ref, bi_ref, lam_ref) = refs[:11]
    rest = refs[12:] if aliased else refs[11:]
    y_ref, cst_ref, hl_ref, buf_ref, hc_ref = rest
    t = pl.program_id(2)
    tc = xa_ref.shape[1]

    @pl.when(t == 0)
    def _():
        hc_ref[...] = h0_ref[0]

    xc = _conv_step(buf_ref, xa_ref[...], hist_ref, cw_ref, cb_ref, LRU_CONV, t, T)
    xcb = xc.astype(BF16)
    rs, gs = [], []
    for hh in range(tc // LRU_BLOCK):
        xb = xcb[:, hh * LRU_BLOCK:(hh + 1) * LRU_BLOCK]
        rs.append(jnp.dot(xb, wr_ref[hh].astype(BF16), preferred_element_type=F32))
        gs.append(jnp.dot(xb, wi_ref[hh].astype(BF16), preferred_element_type=F32))
    r = jax.nn.sigmoid(jnp.concatenate(rs, axis=-1) + br_ref[...])
    ig = jax.nn.sigmoid(jnp.concatenate(gs, axis=-1) + bi_ref[...])
    log_a = -LRU_C * r * _softplus(-lam_ref[...])
    a = jnp.exp(log_a)
    u = jnp.sqrt(-(jnp.tanh(log_a) * (jnp.exp(2.0 * log_a) + 1.0))) * (ig * xc)

    row = lax.broadcasted_iota(jnp.int32, (T, tc), 0)
    s = 1
    while s < T:
        keep = row >= s
        u = u + a * jnp.where(keep, pltpu.roll(u, s, 0), 0.0)
        a = a * jnp.where(keep, pltpu.roll(a, s, 0), 1.0)
        s *= 2
    h = u + a * hc_ref[...]
    hc_ref[...] = h[T - 1:T, :]
    y_ref[...] = (h * _gelu(ga_ref[...])).astype(BF16)

    @pl.when(t == nt - 1)
    def _():
        cst_ref[0] = buf_ref[T:T + HIST_ROWS, :]
        hl_ref[0] = h[T - 1:T, :]


def _lru_call(proj, y_prev, hist, h0, p, e, *, n_seq, L, T, row0, m):
    nt = L // T
    rb0 = row0 // T
    tc = LRU_TILE
    nc = LRU_WIDTH // tc
    hpt = tc // LRU_BLOCK
    row = lambda s, c, t: rb0 + s * nt + t
    vec = lambda: pl.BlockSpec((None, 1, tc), lambda s, c, t: (e, 0, c))
    in_specs = [
        pl.BlockSpec((T, tc), lambda s, c, t: (row(s, c, t), c)),
        pl.BlockSpec((T, tc), lambda s, c, t: (row(s, c, t), nc + c)),
        pl.BlockSpec((1, HIST_ROWS, tc), lambda s, c, t: (s, 0, c)),
        pl.BlockSpec((1, 1, tc), lambda s, c, t: (s, 0, c)),
        pl.BlockSpec((None, LRU_CONV, tc), lambda s, c, t: (e, 0, c)),
        vec(),
        pl.BlockSpec((None, hpt, LRU_BLOCK, LRU_BLOCK), lambda s, c, t: (e, c, 0, 0)),
        vec(),
        pl.BlockSpec((None, hpt, LRU_BLOCK, LRU_BLOCK), lambda s, c, t: (e, c, 0, 0)),
        vec(),
        vec(),
    ]
    args = [proj, proj, hist, h0, p["lru_conv_w"], p["lru_conv_b"], p["lru_w_r"], p["lru_b_r"], p["lru_w_i"], p["lru_b_i"],
            p["lru_lambda"]]
    aliases = {}
    if y_prev is not None:
        in_specs.append(pl.BlockSpec(memory_space=pl.ANY))
        args.append(y_prev)
        aliases = {len(args) - 1: 0}
    return pl.pallas_call(
        functools.partial(_lru_kernel, T=T, nt=nt, aliased=y_prev is not None),
        name=f"lru_L{L}",
        out_shape=(jax.ShapeDtypeStruct((m, D_MODEL), BF16),
                   jax.ShapeDtypeStruct((n_seq, HIST_ROWS, LRU_WIDTH), F32),
                   jax.ShapeDtypeStruct((n_seq, 1, LRU_WIDTH), F32)),
        grid=(n_seq, nc, nt),
        in_specs=in_specs,
        out_specs=(pl.BlockSpec((T, tc), lambda s, c, t: (row(s, c, t), c)),
                   pl.BlockSpec((1, HIST_ROWS, tc), lambda s, c, t: (s, 0, c)),
                   pl.BlockSpec((1, 1, tc), lambda s, c, t: (s, 0, c))),
        scratch_shapes=[pltpu.VMEM((T + HIST_ROWS, tc), F32), pltpu.VMEM((1, tc), F32)],
        input_output_aliases=aliases,
        compiler_params=pltpu.CompilerParams(dimension_semantics=("arbitrary", "arbitrary", "arbitrary"),
                                             vmem_limit_bytes=VMEM_LIMIT_SMALL),
    )(*args)


def _ssd_kernel(*refs, T, nt):
    (z_ref, x_ref, b_ref, c_ref, dt_ref, hx_ref, hb_ref, hcn_ref, s0_ref,
     cwx_ref, cbx_ref, cwb_ref, cbb_ref, cwc_ref, cbc_ref, alog_ref, dd_ref, nw_ref, _y_prev,
     y_ref, csx_ref, csb_ref, csc_ref, sf_ref, bufx_ref, bufb_ref, bufc_ref, s_ref, yd_ref) = refs
    t = pl.program_id(2)
    P = SSD_HEAD_DIM
    gw, n = GROUP_WIDTH, SSD_STATE
    width = x_ref.shape[1]

    xs_all = _silu(_conv_step(bufx_ref, x_ref[...], hx_ref, cwx_ref, cbx_ref, SSD_CONV, t, T))
    bm_all = _silu(_conv_step(bufb_ref, b_ref[...], hb_ref, cwb_ref, cbb_ref, SSD_CONV, t, T))
    cm_all = _silu(_conv_step(bufc_ref, c_ref[...], hcn_ref, cwc_ref, cbc_ref, SSD_CONV, t, T))

    @pl.when(t == 0)
    def _():
        s_ref[...] = s0_ref[0]

    dt = dt_ref[...]
    adt = -jnp.exp(alog_ref[...]) * dt
    row = lax.broadcasted_iota(jnp.int32, (T, width), 0)
    acs_all = adt
    s = 1
    while s < T:
        acs_all = acs_all + jnp.where(row >= s, pltpu.roll(acs_all, s, 0), 0.0)
        s *= 2
    xd_all = xs_all * dt
    eacs_all = jnp.exp(acs_all)
    last_all = acs_all[T - 1:T, :]
    xdd_all = xd_all * jnp.exp(last_all - acs_all)
    elast_all = jnp.exp(last_all)
    tri = lax.broadcasted_iota(jnp.int32, (T, T), 0) >= lax.broadcasted_iota(jnp.int32, (T, T), 1)

    G = SSD_GROUPS_PER_STEP
    HPL = LANES // P
    gcols = [slice(gg * gw, (gg + 1) * gw) for gg in range(G)]
    bmb = [bm_all[:, gg * n:(gg + 1) * n].astype(BF16) for gg in range(G)]
    cmb = [cm_all[:, gg * n:(gg + 1) * n].astype(BF16) for gg in range(G)]
    xdb_all = xd_all.astype(BF16)
    cb = [lax.dot_general(cmb[gg], bmb[gg], (((1,), (1,)), ((), ())), preferred_element_type=F32) for gg in range(G)]
    st = [s_ref[gg] for gg in range(G)]
    y_off = [jnp.dot(cmb[gg], st[gg].astype(BF16), preferred_element_type=F32) for gg in range(G)]
    st_inc = [lax.dot_general(bmb[gg], xdd_all[:, gcols[gg]].astype(BF16), (((0,), (0,)), ((), ())),
                              preferred_element_type=F32) for gg in range(G)]
    blk_t = []
    for b in range(width // LANES):
        blk = acs_all[:, b * LANES:(b + 1) * LANES]
        if T < LANES:
            blk = jnp.concatenate([blk, jnp.zeros((LANES - T, LANES), F32)], axis=0)
        blk_t.append(blk.T)
    md = []
    for h in range(width // P):
        col = acs_all[:, h * P:h * P + 1]
        rowv = blk_t[h // HPL][(h % HPL) * P:(h % HPL) * P + 1, :T]
        decay = jnp.exp(jnp.where(tri, col - rowv, -jnp.inf))
        md.append((cb[h * P // gw] * decay).astype(BF16))
    for h in range(width // P):
        yd_ref[:, h * P:(h + 1) * P] = jnp.dot(md[h], xdb_all[:, h * P:(h + 1) * P], preferred_element_type=F32)
    for gg in range(G):
        cols = gcols[gg]
        s_ref[gg] = st[gg] * elast_all[:, cols] + st_inc[gg]
        yt = yd_ref[:, cols] + y_off[gg] * eacs_all[:, cols] + dd_ref[:, cols] * xs_all[:, cols]
        yt = yt * _silu(z_ref[:, cols])
        yt = yt * lax.rsqrt(jnp.mean(yt * yt, axis=-1, keepdims=True) + NORM_EPS) * nw_ref[:, cols]
        y_ref[:, cols] = yt.astype(BF16)

    @pl.when(t == nt - 1)
    def _():
        csx_ref[0] = bufx_ref[T:T + HIST_ROWS, :]
        csb_ref[0] = bufb_ref[T:T + HIST_ROWS, :]
        csc_ref[0] = bufc_ref[T:T + HIST_ROWS, :]
        sf_ref[0] = s_ref[...]


def _ssd_call(proj, dtr, y_prev, hist, s0, p, e, *, n_seq, L, T, row0, m):
    nt = L // T
    rb0 = row0 // T
    gs = SSD_GROUPS_PER_STEP
    G = SSD_GROUPS
    gw = gs * GROUP_WIDTH
    n = gs * SSD_STATE
    row = lambda s, g, t: rb0 + s * nt + t
    zb = OFF_SSD_Z // gw
    xb = OFF_SSD_XBC // gw
    bb = (OFF_SSD_XBC + SSD_INNER) // n
    cb = (OFF_SSD_XBC + SSD_INNER + SSD_BC) // n
    hb = SSD_INNER // n
    hc = (SSD_INNER + SSD_BC) // n
    vec = lambda: pl.BlockSpec((None, 1, gw), lambda s, g, t: (e, 0, g))
    in_specs = [
        pl.BlockSpec((T, gw), lambda s, g, t: (row(s, g, t), zb + g)),
        pl.BlockSpec((T, gw), lambda s, g, t: (row(s, g, t), xb + g)),
        pl.BlockSpec((T, n), lambda s, g, t: (row(s, g, t), bb + g)),
        pl.BlockSpec((T, n), lambda s, g, t: (row(s, g, t), cb + g)),
        pl.BlockSpec((T, gw), lambda s, g, t: (row(s, g, t), g)),
        pl.BlockSpec((1, HIST_ROWS, gw), lambda s, g, t: (s, 0, g)),
        pl.BlockSpec((1, HIST_ROWS, n), lambda s, g, t: (s, 0, hb + g)),
        pl.BlockSpec((1, HIST_ROWS, n), lambda s, g, t: (s, 0, hc + g)),
        pl.BlockSpec((1, gs, SSD_STATE, GROUP_WIDTH), lambda s, g, t: (s, g, 0, 0)),
        pl.BlockSpec((None, SSD_CONV, gw), lambda s, g, t: (e, 0, g)),
        pl.BlockSpec((None, 1, gw), lambda s, g, t: (e, 0, g)),
        pl.BlockSpec((None, SSD_CONV, n), lambda s, g, t: (e, 0, hb + g)),
        pl.BlockSpec((None, 1, n), lambda s, g, t: (e, 0, hb + g)),
        pl.BlockSpec((None, SSD_CONV, n), lambda s, g, t: (e, 0, hc + g)),
        pl.BlockSpec((None, 1, n), lambda s, g, t: (e, 0, hc + g)),
        vec(), vec(), vec(),
        pl.BlockSpec(memory_space=pl.ANY),
    ]
    args = [proj, proj, proj, proj, dtr, hist, hist, hist, s0,
            p["ssd_conv_w"], p["ssd_conv_b"], p["ssd_conv_w"], p["ssd_conv_b"], p["ssd_conv_w"], p["ssd_conv_b"],
            p["ssd_a_log_x"], p["ssd_d_x"], p["ssd_norm_w"], y_prev]
    yb = LRU_WIDTH // gw
    return pl.pallas_call(
        functools.partial(_ssd_kernel, T=T, nt=nt),
        name=f"ssd_L{L}",
        out_shape=(jax.ShapeDtypeStruct((m, D_MODEL), BF16),
                   jax.ShapeDtypeStruct((n_seq, HIST_ROWS, SSD_INNER), F32),
                   jax.ShapeDtypeStruct((n_seq, HIST_ROWS, SSD_BC), F32),
                   jax.ShapeDtypeStruct((n_seq, HIST_ROWS, SSD_BC), F32),
                   jax.ShapeDtypeStruct((n_seq, G, SSD_STATE, GROUP_WIDTH), F32)),
        grid=(n_seq, G // gs, nt),
        in_specs=in_specs,
        out_specs=(pl.BlockSpec((T, gw), lambda s, g, t: (row(s, g, t), yb + g)),
                   pl.BlockSpec((1, HIST_ROWS, gw), lambda s, g, t: (s, 0, g)),
                   pl.BlockSpec((1, HIST_ROWS, n), lambda s, g, t: (s, 0, g)),
                   pl.BlockSpec((1, HIST_ROWS, n), lambda s, g, t: (s, 0, g)),
                   pl.BlockSpec((1, gs, SSD_STATE, GROUP_WIDTH), lambda s, g, t: (s, g, 0, 0))),
        scratch_shapes=[pltpu.VMEM((T + HIST_ROWS, gw), F32), pltpu.VMEM((T + HIST_ROWS, n), F32),
                        pltpu.VMEM((T + HIST_ROWS, n), F32), pltpu.VMEM((gs, SSD_STATE, GROUP_WIDTH), F32),
                        pltpu.VMEM((T, gw), F32)],
        input_output_aliases={len(args) - 1: 0},
        compiler_params=pltpu.CompilerParams(dimension_semantics=("arbitrary", "arbitrary", "arbitrary"),
                                             vmem_limit_bytes=VMEM_LIMIT_SMALL),
    )(*args)


def _ret_kernel(*refs, nt, n_aliased):
    (q_ref, k_ref, v_ref, g_ref, cos_ref, sin_ref, dm_ref, cr_ref, kd_ref, cd_ref, s0_ref) = refs[:11]
    y_ref, sf_ref, s_ref = refs[11 + n_aliased:]
    t = pl.program_id(2)
    half = RET_QK_DIM // 2
    dk, dv = RET_QK_DIM, RET_V_DIM

    @pl.when(t == 0)
    def _():
        s_ref[...] = s0_ref[0]

    cos = cos_ref[...]
    sin = sin_ref[...]

    def rot(x):
        x1 = x[:, :half]
        x2 = x[:, half:]
        return jnp.concatenate([x1 * cos - x2 * sin, x2 * cos + x1 * sin], axis=-1)

    for hh in range(RET_HEADS_PER_STEP):
        q = rot(q_ref[:, hh * dk:(hh + 1) * dk].astype(F32))
        k = rot(k_ref[:, hh * dk:(hh + 1) * dk].astype(F32)) * (RET_QK_DIM ** -0.5)
        qb = q.astype(BF16)
        vb = v_ref[:, hh * dv:(hh + 1) * dv].astype(BF16)
        att = lax.dot_general(qb, k.astype(BF16), (((1,), (1,)), ((), ())), preferred_element_type=F32) * dm_ref[hh]
        st = s_ref[hh]
        o = (jnp.dot(att.astype(BF16), vb, preferred_element_type=F32)
             + jnp.dot((q * cr_ref[hh]).astype(BF16), st.astype(BF16), preferred_element_type=F32))
        st_new = st * cd_ref[hh][0:1, :] + lax.dot_general((k * kd_ref[hh]).astype(BF16), vb, (((0,), (0,)), ((), ())),
                                                            preferred_element_type=F32)
        s_ref[hh] = st_new
        o = o * lax.rsqrt(jnp.mean(o * o, axis=-1, keepdims=True) + NORM_EPS)
        y_ref[:, hh * dv:(hh + 1) * dv] = (_silu(g_ref[:, hh * dv:(hh + 1) * dv].astype(F32)) * o).astype(BF16)

    @pl.when(t == nt - 1)
    def _():
        sf_ref[0] = s_ref[...]


def _ret_tables(T, ct, pos0, L):
    f32 = F32
    log_gamma = jnp.log1p(-jnp.exp2(-5.0 - jnp.arange(RET_HEADS, dtype=f32)))
    n = jnp.arange(T, dtype=f32)
    chunk = jnp.arange(T) // ct
    same = chunk[:, None] == chunk[None, :]
    earlier = chunk[None, :] < chunk[:, None]
    diff = n[:, None] - n[None, :]
    lg = log_gamma[:, None, None]
    dm = jnp.where(same[None], jnp.exp(jnp.abs(diff)[None] * lg), jnp.where(earlier[None], jnp.exp(diff[None] * lg), 0.0))
    cross = jnp.exp((n[None, :] + 1.0) * log_gamma[:, None])
    kvd = jnp.exp((T - 1.0 - n)[None, :] * log_gamma[:, None])
    cdec = jnp.exp(T * log_gamma)
    cr = jnp.broadcast_to(cross[:, :, None], (RET_HEADS, T, RET_QK_DIM))
    kd = jnp.broadcast_to(kvd[:, :, None], (RET_HEADS, T, RET_QK_DIM))
    cd = jnp.broadcast_to(cdec[:, None, None], (RET_HEADS, SUBLANES, RET_V_DIM))
    half = RET_QK_DIM // 2
    inv = 1.0 / (ROPE_BASE ** jnp.linspace(0.0, 1.0, half, dtype=f32))
    pos = float(pos0) + jnp.arange(L, dtype=f32)
    ang = pos[:, None] * inv[None, :]
    return dm, cr, kd, cd, jnp.cos(ang), jnp.sin(ang)


def _ret_call(proj, y_prev, s0, e0, sf_prev, e, n_layers, *, n_seq, L, T, row0, pos0, m):
    nt = L // T
    rb0 = row0 // T
    ct = min(CHUNK, L)
    assert T % ct == 0
    dm, cr, kd, cd, cos, sin = _ret_tables(T, ct, pos0, L)
    H = RET_HEADS
    hs = RET_HEADS_PER_STEP
    dk, dv = hs * RET_QK_DIM, hs * RET_V_DIM
    row = lambda s, h, t: rb0 + s * nt + t
    kb0 = RET_QK_WIDTH // dk
    vb0 = 2 * RET_QK_WIDTH // dv
    gb0 = (2 * RET_QK_WIDTH + RET_V_WIDTH) // dv
    in_specs = [
        pl.BlockSpec((T, dk), lambda s, h, t: (row(s, h, t), h)),
        pl.BlockSpec((T, dk), lambda s, h, t: (row(s, h, t), kb0 + h)),
        pl.BlockSpec((T, dv), lambda s, h, t: (row(s, h, t), vb0 + h)),
        pl.BlockSpec((T, dv), lambda s, h, t: (row(s, h, t), gb0 + h)),
        pl.BlockSpec((T, RET_QK_DIM // 2), lambda s, h, t: (t, 0)),
        pl.BlockSpec((T, RET_QK_DIM // 2), lambda s, h, t: (t, 0)),
        pl.BlockSpec((hs, T, T), lambda s, h, t: (h, 0, 0)),
        pl.BlockSpec((hs, T, RET_QK_DIM), lambda s, h, t: (h, 0, 0)),
        pl.BlockSpec((hs, T, RET_QK_DIM), lambda s, h, t: (h, 0, 0)),
        pl.BlockSpec((hs, SUBLANES, RET_V_DIM), lambda s, h, t: (h, 0, 0)),
        pl.BlockSpec((None, 1, hs, RET_QK_DIM, RET_V_DIM), lambda s, h, t: (e0, s, h, 0, 0)),
    ]
    args = [proj, proj, proj, proj, cos, sin, dm, cr, kd, cd, s0]
    aliases = {}
    for out_idx, prev in ((0, y_prev), (1, sf_prev)):
        if prev is not None:
            in_specs.append(pl.BlockSpec(memory_space=pl.ANY))
            args.append(prev)
            aliases[len(args) - 1] = out_idx
    return pl.pallas_call(
        functools.partial(_ret_kernel, nt=nt, n_aliased=len(aliases)),
        name=f"ret_L{L}",
        out_shape=(jax.ShapeDtypeStruct((m, RET_V_WIDTH), BF16),
                   jax.ShapeDtypeStruct((n_layers, n_seq, H, RET_QK_DIM, RET_V_DIM), F32)),
        grid=(n_seq, H // hs, nt),
        in_specs=in_specs,
        out_specs=(pl.BlockSpec((T, dv), lambda s, h, t: (row(s, h, t), h)),
                   pl.BlockSpec((None, 1, hs, RET_QK_DIM, RET_V_DIM), lambda s, h, t: (e, s, h, 0, 0))),
        scratch_shapes=[pltpu.VMEM((hs, RET_QK_DIM, RET_V_DIM), F32)],
        input_output_aliases=aliases,
        compiler_params=pltpu.CompilerParams(dimension_semantics=("arbitrary", "arbitrary", "arbitrary"),
                                             vmem_limit_bytes=VMEM_LIMIT_SMALL),
    )(*args)


def _pad_hist(state):
    return jnp.pad(state, ((0, 0), (HIST_ROWS - state.shape[1], 0), (0, 0)))


def _ssd_state_in(s):
    n = s.shape[0]
    s = s.reshape(n, SSD_GROUPS, HEADS_PER_GROUP, SSD_HEAD_DIM, SSD_STATE)
    return s.transpose(0, 1, 4, 2, 3).reshape(n, SSD_GROUPS, SSD_STATE, GROUP_WIDTH)


def _ssd_state_out(s):
    n = s.shape[0]
    s = s.reshape(n, SSD_GROUPS, SSD_STATE, HEADS_PER_GROUP, SSD_HEAD_DIM)
    return s.transpose(0, 1, 3, 4, 2).reshape(n, SSD_HEADS, SSD_HEAD_DIM, SSD_STATE)


def kernel(x_prompt, x_sample, state_lru_conv, state_lru_h, state_ssd_conv, state_ssd, state_ret, state_ffn_conv,
           norm_mix_pre, norm_mix_post, norm_ffn_pre, norm_ffn_post, w_in_ab, lru_conv_w, lru_conv_b, lru_w_r, lru_b_r,
           lru_w_i, lru_b_i, lru_lambda, ssd_conv_w, ssd_conv_b, ssd_dt_bias, ssd_a_log, ssd_d, ssd_norm_w, w_out_ab,
           w_in_ret, w_out_ret, ffn_w_up, ffn_conv_w, ffn_conv_b, ffn_w_down):
    bp, lp, d = x_prompt.shape
    bs, ls, _ = x_sample.shape
    depth = norm_mix_pre.shape[0]
    n_even = w_in_ab.shape[0]
    mp = bp * lp
    m = mp + bs * ls

    groups = ((bp, lp, 0, 0), (bs, ls, mp, PAST_LEN))
    t_lru = (min(256, lp), ls)
    t_ssd = (min(128, lp), ls)
    t_ret = (min(256, lp), ls)

    rep = lambda v: jnp.repeat(v, SSD_HEAD_DIM, axis=-1).reshape(n_even, 1, SSD_INNER)
    vec3 = lambda v: v.reshape(v.shape[0], 1, v.shape[-1])
    ab = {
        "lru_conv_w": lru_conv_w, "lru_conv_b": vec3(lru_conv_b), "lru_w_r": lru_w_r, "lru_b_r": vec3(lru_b_r),
        "lru_w_i": lru_w_i, "lru_b_i": vec3(lru_b_i), "lru_lambda": vec3(lru_lambda),
        "ssd_conv_w": ssd_conv_w, "ssd_conv_b": vec3(ssd_conv_b),
        "ssd_a_log_x": rep(ssd_a_log), "ssd_d_x": rep(ssd_d),
        "ssd_norm_w": vec3(ssd_norm_w),
    }
    ffn_hist = _pad_hist(state_ffn_conv.reshape(depth * bs, FFN_CONV - 1, D_FF)).reshape(depth, bs, HIST_ROWS, D_FF)
    ffn_hist = jnp.pad(ffn_hist, ((0, 0), (bp, 0), (0, 0), (0, 0)))

    new = {k: ([], []) for k in ("lc", "lh", "sc", "ss", "fc")}
    ret_stack = [None, None]
    w_in_ab_t = jnp.swapaxes(w_in_ab, 1, 2)

    x, hn, dtr = _first_norm(x_prompt.reshape(mp, d), x_sample.reshape(bs * ls, d), norm_mix_pre[0], w_in_ab_t, ssd_dt_bias)
    for layer in range(depth):
        e = layer // 2
        if layer % 2 == 0:
            proj = _matmul(hn, w_in_ab_t, e, n_cols=OFF_SSD_DT, w_transposed=True)
            y = None
            for gi, (n_seq, L, row0, _) in enumerate(groups):
                if gi == 0:
                    hist = jnp.zeros((n_seq, HIST_ROWS, LRU_WIDTH), F32)
                    h0 = jnp.zeros((n_seq, 1, LRU_WIDTH), F32)
                else:
                    hist = _pad_hist(state_lru_conv[e])
                    h0 = state_lru_h[e].reshape(n_seq, 1, LRU_WIDTH)
                y, cst, hl = _lru_call(proj, y, hist, h0, ab, e, n_seq=n_seq, L=L, T=t_lru[gi], row0=row0, m=m)
                new["lc"][gi].append(cst[:, HIST_ROWS - (LRU_CONV - 1):, :])
                new["lh"][gi].append(hl.reshape(n_seq, LRU_WIDTH))
            for gi, (n_seq, L, row0, _) in enumerate(groups):
                if gi == 0:
                    hist = jnp.zeros((n_seq, HIST_ROWS, SSD_CONV_DIM), F32)
                    s0 = jnp.zeros((n_seq, SSD_GROUPS, SSD_STATE, GROUP_WIDTH), F32)
                else:
                    hist = _pad_hist(state_ssd_conv[e])
                    s0 = _ssd_state_in(state_ssd[e])
                y, csx, csb, csc, sf = _ssd_call(proj, dtr, y, hist, s0, ab, e, n_seq=n_seq, L=L, T=t_ssd[gi], row0=row0, m=m)
                cst = jnp.concatenate([csx, csb, csc], axis=-1)
                new["sc"][gi].append(cst[:, HIST_ROWS - (SSD_CONV - 1):, :])
                new["ss"][gi].append(_ssd_state_out(sf))
            mix = _matmul(y, w_out_ab, e, out_dtype=BF16)
        else:
            proj = _matmul(hn, w_in_ret, e, out_dtype=BF16)
            y = None
            for gi, (n_seq, L, row0, pos0) in enumerate(groups):
                if gi == 0:
                    s0, e0 = jnp.zeros((1, n_seq, RET_HEADS, RET_QK_DIM, RET_V_DIM), F32), 0
                else:
                    s0, e0 = state_ret, e
                y, ret_stack[gi] = _ret_call(proj, y, s0, e0, ret_stack[gi], e, depth // 2, n_seq=n_seq, L=L, T=t_ret[gi],
                                             row0=row0, pos0=pos0, m=m)
            mix = _matmul(y, w_out_ret, e, out_dtype=BF16)
        x, hn = _resid_norm(x, mix, norm_mix_post[layer], norm_ffn_pre[layer])
        h, fst = _ffn_up(hn, ffn_w_up, ffn_conv_w, ffn_conv_b, ffn_hist[layer], layer, bp=bp, lp=lp, bs=bs, ls=ls)
        fst = fst[:, HIST_ROWS - (FFN_CONV - 1):, :]
        new["fc"][0].append(fst[:bp])
        new["fc"][1].append(fst[bp:])
        f = _matmul(h, ffn_w_down, layer, out_dtype=BF16)
        if layer + 1 < depth:
            if (layer + 1) % 2 == 0:
                x, hn, dtr = _resid_norm(x, f, norm_ffn_post[layer], norm_mix_pre[layer + 1],
                                         dt_of=(w_in_ab_t, ssd_dt_bias, (layer + 1) // 2))
            else:
                x, hn = _resid_norm(x, f, norm_ffn_post[layer], norm_mix_pre[layer + 1])
        else:
            y_prompt, y_sample = _last_resid(x, f, norm_ffn_post[layer], mp)

    outs = [y_prompt.reshape(bp, lp, d), y_sample.reshape(bs, ls, d)]
    for gi in range(2):
        for k in ("lc", "lh", "sc", "ss", "rs", "fc"):
            outs.append(ret_stack[gi] if k == "rs" else jnp.stack(new[k][gi]))
    return tuple(outs)
```

```python
import functools

import jax
import jax.numpy as jnp
from jax import lax
from jax.experimental import pallas as pl
from jax.experimental.pallas import tpu as pltpu

F32 = jnp.float32
BF16 = jnp.bfloat16

D_MODEL = 4096
PAST_LEN = 1024
CHUNK = 64
NORM_EPS = 1e-6
LRU_WIDTH = D_MODEL // 2
LRU_HEADS = 16
LRU_BLOCK = LRU_WIDTH // LRU_HEADS
LRU_CONV = 4
LRU_C = 8.0
SSD_INNER = D_MODEL // 2
SSD_HEAD_DIM = 64
SSD_HEADS = SSD_INNER // SSD_HEAD_DIM
SSD_GROUPS = 8
SSD_STATE = 128
SSD_CONV = 4
SSD_BC = SSD_GROUPS * SSD_STATE
SSD_CONV_DIM = SSD_INNER + 2 * SSD_BC
OFF_SSD_Z = 2 * LRU_WIDTH
OFF_SSD_XBC = OFF_SSD_Z + SSD_INNER
OFF_SSD_DT = OFF_SSD_XBC + SSD_CONV_DIM
RET_HEADS = 16
RET_QK_DIM = D_MODEL // RET_HEADS
RET_V_DIM = 2 * RET_QK_DIM
RET_QK_WIDTH = RET_HEADS * RET_QK_DIM
RET_V_WIDTH = RET_HEADS * RET_V_DIM
ROPE_BASE = 10000.0
D_FF = 3 * D_MODEL
FFN_CONV = 3

SUBLANES = 8
LANES = 128
BF16_ROWS = 16
HIST_ROWS = SUBLANES
MIB = 1 << 20
VMEM_LIMIT_BIG = 56 * MIB
VMEM_LIMIT_SMALL = 40 * MIB

HEADS_PER_GROUP = SSD_HEADS // SSD_GROUPS
GROUP_WIDTH = HEADS_PER_GROUP * SSD_HEAD_DIM
LRU_TILE = 512
MATMUL_TM_CAP = 1056
MATMUL_TK = 4096
MATMUL_TN = 1024
FFN_TILE = 1024
FFN_COLS = 512
NORM_ROWS = 256
RET_HEADS_PER_STEP = 4
SSD_GROUPS_PER_STEP = 4


def _pick_tile(total, cap, mult):
    best = None
    for t in range(mult, min(total, cap) + 1, mult):
        if total % t == 0:
            best = t
    assert best is not None, (total, cap, mult)
    return best


def _silu(x):
    return x * jax.nn.sigmoid(x)


def _gelu(x):
    return jax.nn.gelu(x, approximate=True)


def _softplus(x):
    return jnp.maximum(x, 0.0) + jnp.log1p(jnp.exp(-jnp.abs(x)))


def _rms(x, w):
    return x * lax.rsqrt(jnp.mean(x * x, axis=-1, keepdims=True) + NORM_EPS) * w


def _emit_dt(hn, wdt_ref, bdt_ref, dt_ref):
    rows = hn.shape[0]
    dt = lax.dot_general(hn, wdt_ref[...].astype(BF16), (((1,), (1,)), ((), ())), preferred_element_type=F32)
    dt = _softplus(dt + bdt_ref[...])
    for h in range(SSD_HEADS):
        dt_ref[:, h * SSD_HEAD_DIM:(h + 1) * SSD_HEAD_DIM] = jnp.broadcast_to(dt[:, h:h + 1], (rows, SSD_HEAD_DIM))


def _resid_norm_kernel(*refs, with_dt):
    if with_dt:
        x_ref, m_ref, wp_ref, wn_ref, wdt_ref, bdt_ref, xo_ref, hn_ref, dt_ref = refs
    else:
        x_ref, m_ref, wp_ref, wn_ref, xo_ref, hn_ref = refs
    xn = x_ref[...] + _rms(m_ref[...].astype(F32), wp_ref[...])
    xo_ref[...] = xn
    hn = _rms(xn, wn_ref[...]).astype(BF16)
    hn_ref[...] = hn
    if with_dt:
        _emit_dt(hn, wdt_ref, bdt_ref, dt_ref)


def _first_norm_kernel(xp_ref, xs_ref, w_ref, wdt_ref, bdt_ref, xo_ref, hn_ref, dt_ref, *, n_ptiles):
    i = pl.program_id(0)

    def emit(x):
        xo_ref[...] = x
        hn = _rms(x, w_ref[...]).astype(BF16)
        hn_ref[...] = hn
        _emit_dt(hn, wdt_ref, bdt_ref, dt_ref)

    @pl.when(i < n_ptiles)
    def _():
        emit(xp_ref[...])

    @pl.when(i >= n_ptiles)
    def _():
        emit(xs_ref[...])


def _last_resid_kernel(x_ref, m_ref, wp_ref, yp_ref, ys_ref, *, n_ptiles):
    i = pl.program_id(0)
    xn = x_ref[...] + _rms(m_ref[...].astype(F32), wp_ref[...])

    @pl.when(i < n_ptiles)
    def _():
        yp_ref[...] = xn

    @pl.when(i >= n_ptiles)
    def _():
        ys_ref[...] = xn


def _row_tile(mp, ms):
    tr = _pick_tile(mp, NORM_ROWS, BF16_ROWS)
    assert ms % tr == 0
    return tr


def _dt_specs(w_t, dt_bias, layer, tr):
    d = w_t.shape[-1]
    assert OFF_SSD_DT % SSD_HEADS == 0
    in_specs = [pl.BlockSpec((None, SSD_HEADS, d), lambda i: (layer, OFF_SSD_DT // SSD_HEADS, 0)),
                pl.BlockSpec((None, 1, SSD_HEADS), lambda i: (layer, 0, 0))]
    args = [w_t, dt_bias.reshape(dt_bias.shape[0], 1, SSD_HEADS)]
    return in_specs, args, pl.BlockSpec((tr, SSD_INNER), lambda i: (i, 0))


def _first_norm(xp, xs, w, w_t, dt_bias):
    mp, d = xp.shape
    ms = xs.shape[0]
    tr = _row_tile(mp, ms)
    n_ptiles = mp // tr
    row_spec = pl.BlockSpec((tr, d), lambda i: (i, 0))
    dt_in_specs, dt_args, dt_out_spec = _dt_specs(w_t, dt_bias, 0, tr)
    return pl.pallas_call(
        functools.partial(_first_norm_kernel, n_ptiles=n_ptiles),
        name="first_norm",
        out_shape=(jax.ShapeDtypeStruct((mp + ms, d), F32), jax.ShapeDtypeStruct((mp + ms, d), BF16),
                   jax.ShapeDtypeStruct((mp + ms, SSD_INNER), F32)),
        grid=((mp + ms) // tr,),
        in_specs=[pl.BlockSpec((tr, d), lambda i: (jnp.minimum(i, n_ptiles - 1), 0)),
                  pl.BlockSpec((tr, d), lambda i: (jnp.maximum(i - n_ptiles, 0), 0)),
                  pl.BlockSpec((1, d), lambda i: (0, 0))] + dt_in_specs,
        out_specs=(row_spec, row_spec, dt_out_spec),
        compiler_params=pltpu.CompilerParams(dimension_semantics=("arbitrary",), vmem_limit_bytes=VMEM_LIMIT_SMALL),
    )(xp, xs, w.reshape(1, d), *dt_args)


def _last_resid(x, mix, w, mp):
    m, d = x.shape
    ms = m - mp
    tr = _row_tile(mp, ms)
    n_ptiles = mp // tr
    row_spec = pl.BlockSpec((tr, d), lambda i: (i, 0))
    return pl.pallas_call(
        functools.partial(_last_resid_kernel, n_ptiles=n_ptiles),
        name="last_resid",
        out_shape=(jax.ShapeDtypeStruct((mp, d), F32), jax.ShapeDtypeStruct((ms, d), F32)),
        grid=(m // tr,),
        in_specs=[row_spec, row_spec, pl.BlockSpec((1, d), lambda i: (0, 0))],
        out_specs=(pl.BlockSpec((tr, d), lambda i: (jnp.minimum(i, n_ptiles - 1), 0)),
                   pl.BlockSpec((tr, d), lambda i: (jnp.maximum(i - n_ptiles, 0), 0))),
        compiler_params=pltpu.CompilerParams(dimension_semantics=("arbitrary",), vmem_limit_bytes=VMEM_LIMIT_SMALL),
    )(x, mix, w.reshape(1, d))


def _resid_norm(x, mix, w_post, w_next, dt_of=None):
    m, d = x.shape
    tr = _pick_tile(m, NORM_ROWS, BF16_ROWS)
    row_spec = pl.BlockSpec((tr, d), lambda i: (i, 0))
    w_spec = pl.BlockSpec((1, d), lambda i: (0, 0))
    in_specs = [row_spec, row_spec, w_spec, w_spec]
    args = [x, mix, w_post.reshape(1, d), w_next.reshape(1, d)]
    out_shape = [jax.ShapeDtypeStruct((m, d), F32), jax.ShapeDtypeStruct((m, d), BF16)]
    out_specs = [row_spec, row_spec]
    if dt_of is not None:
        dt_in_specs, dt_args, dt_out_spec = _dt_specs(*dt_of, tr)
        in_specs += dt_in_specs
        args += dt_args
        out_shape.append(jax.ShapeDtypeStruct((m, SSD_INNER), F32))
        out_specs.append(dt_out_spec)
    return pl.pallas_call(
        functools.partial(_resid_norm_kernel, with_dt=dt_of is not None),
        name="resid_norm",
        out_shape=tuple(out_shape),
        grid=(m // tr,),
        in_specs=in_specs,
        out_specs=tuple(out_specs),
        compiler_params=pltpu.CompilerParams(dimension_semantics=("arbitrary",), vmem_limit_bytes=VMEM_LIMIT_SMALL),
    )(*args)


def _matmul_kernel(a_ref, w_ref, o_ref, wb0_ref, wb1_ref, *, nq, ck, w_transposed):
    q = pl.program_id(0)
    i = pl.program_id(1)
    wb_refs = (wb0_ref, wb1_ref)

    def stage(par):
        w = w_ref[...]
        if w_transposed:
            w = w.T
        wb_refs[par][pl.ds(pl.multiple_of(i * ck, BF16_ROWS), ck), :] = w.astype(BF16)

    def product(par):
        o_ref[...] = jnp.dot(a_ref[...], wb_refs[par][...], preferred_element_type=F32).astype(o_ref.dtype)

    @pl.when(q == 0)
    def _():
        stage(0)

    for par in (0, 1):
        @pl.when(jnp.logical_and(jnp.logical_and(q >= 1, q < nq), q % 2 == par))
        def _():
            stage(par)
            product(1 - par)

    @pl.when(q == nq)
    def _():
        product((nq - 1) % 2)


def _matmul(a, w, layer, *, n_cols=None, out_dtype=F32, w_transposed=False):
    m, k = a.shape
    n = n_cols if n_cols is not None else w.shape[-2 if w_transposed else -1]
    depth_ratio = max(1, k // MATMUL_TK)
    tm = _pick_tile(m, MATMUL_TM_CAP // depth_ratio, BF16_ROWS)
    tn = MATMUL_TN if depth_ratio == 1 else MATMUL_TN // 2
    n_mt = m // tm
    ck = k // n_mt
    assert n % tn == 0 and k % n_mt == 0 and ck % LANES == 0
    nq = n // tn

    def a_map(q, i):
        return (jnp.where(q >= 1, i, 0), 0)

    def w_map(q, i):
        row = jnp.where(q < nq, i, n_mt - 1)
        p = jnp.minimum(q, nq - 1)
        return (layer, p, row) if w_transposed else (layer, row, p)

    def o_map(q, i):
        return (jnp.where(q >= 1, i, 0), jnp.maximum(q - 1, 0))

    return pl.pallas_call(
        functools.partial(_matmul_kernel, nq=nq, ck=ck, w_transposed=w_transposed),
        name=f"matmul_k{k}_n{n}",
        out_shape=jax.ShapeDtypeStruct((m, n), out_dtype),
        grid=(nq + 1, n_mt),
        in_specs=[
            pl.BlockSpec((tm, k), a_map),
            pl.BlockSpec((None, tn, ck) if w_transposed else (None, ck, tn), w_map),
        ],
        out_specs=pl.BlockSpec((tm, tn), o_map),
        scratch_shapes=[pltpu.VMEM((k, tn), BF16), pltpu.VMEM((k, tn), BF16)],
        compiler_params=pltpu.CompilerParams(
            dimension_semantics=("arbitrary", "arbitrary"), vmem_limit_bytes=VMEM_LIMIT_BIG),
    )(a, w)


def _conv_from_buf(buf_ref, cw_ref, cb_ref, width, rows):
    v = buf_ref[0:HIST_ROWS + rows, :]

    def tap(k):
        back = width - 1 - k
        return v[HIST_ROWS:, :] if back == 0 else pltpu.roll(v, back, 0)[HIST_ROWS:, :]

    y = cb_ref[...] + tap(0) * cw_ref[0:1, :]
    for k in range(1, width):
        y = y + tap(k) * cw_ref[k:k + 1, :]
    return y


def _conv_step(buf_ref, src, hist_ref, cw_ref, cb_ref, width, t, rows):
    @pl.when(t == 0)
    def _():
        buf_ref[0:HIST_ROWS, :] = hist_ref[0]

    @pl.when(t > 0)
    def _():
        buf_ref[0:HIST_ROWS, :] = buf_ref[rows:rows + HIST_ROWS, :]

    buf_ref[HIST_ROWS:HIST_ROWS + rows, :] = src
    return _conv_from_buf(buf_ref, cw_ref, cb_ref, width, rows)


def _ffn_up_kernel(a_ref, wg_ref, wu_ref, cw_ref, cb_ref, hist_ref, h_ref, st_ref, wb0_ref, wb1_ref, buf_ref, *,
                   tiles_per_seq, n_ptiles, n_pseq, n_sseq, ls, tf, nq):
    q = pl.program_id(0)
    i = pl.program_id(1)
    tm = a_ref.shape[0]
    ck = wg_ref.shape[0]
    wb_refs = (wb0_ref, wb1_ref)
    H = HIST_ROWS

    def stage(par):
        rows = pl.ds(pl.multiple_of(i * ck, BF16_ROWS), ck)
        wb_refs[par][rows, 0:tf] = wg_ref[...].astype(BF16)
        wb_refs[par][rows, tf:2 * tf] = wu_ref[...].astype(BF16)

    def prompt_tile(par):
        p = jnp.dot(a_ref[...], wb_refs[par][...], preferred_element_type=F32)
        seq = i // tiles_per_seq
        first = (i % tiles_per_seq) == 0
        buf_ref[0:H, :] = jnp.where(first, hist_ref[seq], buf_ref[tm:tm + H, :])
        buf_ref[H:H + tm, :] = p[:, 0:tf]
        gate = _gelu(_conv_from_buf(buf_ref, cw_ref, cb_ref, FFN_CONV, tm))
        h_ref[...] = (gate * p[:, tf:2 * tf]).astype(BF16)
        st_ref[seq] = buf_ref[tm:tm + H, :]

    def sample_tile(par):
        rows = n_sseq * ls
        p = jnp.dot(a_ref[0:rows, :], wb_refs[par][...], preferred_element_type=F32)
        for s in range(n_sseq):
            lo = s * ls
            buf_ref[0:H, :] = hist_ref[n_pseq + s]
            buf_ref[H:H + ls, :] = p[lo:lo + ls, 0:tf]
            gate = _gelu(_conv_from_buf(buf_ref, cw_ref, cb_ref, FFN_CONV, ls))
            h_ref[lo:lo + ls, :] = (gate * p[lo:lo + ls, tf:2 * tf]).astype(BF16)
            st_ref[n_pseq + s] = buf_ref[ls:ls + H, :]

    is_prompt = i < n_ptiles

    @pl.when(jnp.logical_and(q == 0, i == 0))
    def _():
        buf_ref[tm:tm + H, :] = jnp.zeros((H, tf), F32)

    @pl.when(jnp.logical_and(q == 0, is_prompt))
    def _():
        stage(0)

    for par in (0, 1):
        @pl.when(jnp.logical_and(jnp.logical_and(q >= 1, q < nq), jnp.logical_and(q % 2 == par, is_prompt)))
        def _():
            stage(par)
            prompt_tile(1 - par)

        @pl.when(jnp.logical_and(jnp.logical_and(q >= 1, (q - 1) % 2 == par), i == n_ptiles))
        def _():
            sample_tile(par)

    @pl.when(jnp.logical_and(q == nq, is_prompt))
    def _():
        prompt_tile((nq - 1) % 2)


def _ffn_up(hn, w_up, conv_w, conv_b, hist, layer, *, bp, lp, bs, ls):
    m, d = hn.shape
    tf = FFN_COLS
    tm = min(FFN_TILE, lp)
    n_ptiles = bp * lp // tm
    ck = d // n_ptiles
    assert lp % tm == 0 and bs * ls <= tm and ls % BF16_ROWS == 0 and d % n_ptiles == 0 and ck % BF16_ROWS == 0
    nseq = bp + bs
    nq = D_FF // tf
    kern = functools.partial(_ffn_up_kernel, tiles_per_seq=lp // tm, n_ptiles=n_ptiles, n_pseq=bp, n_sseq=bs, ls=ls,
                             tf=tf, nq=nq)
    prev = lambda q: jnp.maximum(q - 1, 0)

    def w_row(q, i):
        return jnp.where(q < nq, jnp.minimum(i, n_ptiles - 1), n_ptiles - 1)

    return pl.pallas_call(
        kern,
        name="ffn_up",
        out_shape=(jax.ShapeDtypeStruct((m, D_FF), BF16), jax.ShapeDtypeStruct((nseq, HIST_ROWS, D_FF), F32)),
        grid=(nq + 1, n_ptiles + 1),
        in_specs=[
            pl.BlockSpec((tm, d), lambda q, i: (jnp.where(q >= 1, i, 0), 0)),
            pl.BlockSpec((None, ck, tf), lambda q, i: (layer, w_row(q, i), jnp.minimum(q, nq - 1))),
            pl.BlockSpec((None, ck, tf), lambda q, i: (layer, w_row(q, i), nq + jnp.minimum(q, nq - 1))),
            pl.BlockSpec((None, FFN_CONV, tf), lambda q, i: (layer, 0, prev(q))),
            pl.BlockSpec((None, 1, tf), lambda q, i: (layer, 0, prev(q))),
            pl.BlockSpec((nseq, HIST_ROWS, tf), lambda q, i: (0, 0, prev(q))),
        ],
        out_specs=(
            pl.BlockSpec((tm, tf), lambda q, i: (jnp.where(q >= 1, i, 0), prev(q))),
            pl.BlockSpec((nseq, HIST_ROWS, tf), lambda q, i: (0, 0, prev(q))),
        ),
        scratch_shapes=[pltpu.VMEM((d, 2 * tf), BF16), pltpu.VMEM((d, 2 * tf), BF16),
                        pltpu.VMEM((tm + HIST_ROWS, tf), F32)],
        compiler_params=pltpu.CompilerParams(dimension_semantics=("arbitrary", "arbitrary"), vmem_limit_bytes=VMEM_LIMIT_BIG),
    )(hn, w_up, w_up, conv_w, conv_b.reshape(conv_b.shape[0], 1, D_FF), hist)


def _lru_kernel(*refs, T, nt, aliased):
    (xa_ref, ga_ref, hist_ref, h0_ref, cw_ref, cb_ref, wr_ref, br_ref, wi_ref, bi_ref, lam_ref) = refs[:11]
    rest = refs[12:] if aliased else refs[11:]
    y_ref, cst_ref, hl_ref, buf_ref, hc_ref = rest
    t = pl.program_id(2)
    tc = xa_ref.shape[1]

    @pl.when(t == 0)
    def _():
        hc_ref[...] = h0_ref[0]

    xc = _conv_step(buf_ref, xa_ref[...], hist_ref, cw_ref, cb_ref, LRU_CONV, t, T)
    xcb = xc.astype(BF16)
    rs, gs = [], []
    for hh in range(tc // LRU_BLOCK):
        xb = xcb[:, hh * LRU_BLOCK:(hh + 1) * LRU_BLOCK]
        rs.append(jnp.dot(xb, wr_ref[hh].astype(BF16), preferred_element_type=F32))
        gs.append(jnp.dot(xb, wi_ref[hh].astype(BF16), preferred_element_type=F32))
    r = jax.nn.sigmoid(jnp.concatenate(rs, axis=-1) + br_ref[...])
    ig = jax.nn.sigmoid(jnp.concatenate(gs, axis=-1) + bi_ref[...])
    log_a = -LRU_C * r * _softplus(-lam_ref[...])
    a = jnp.exp(log_a)
    u = jnp.sqrt(-(jnp.tanh(log_a) * (jnp.exp(2.0 * log_a) + 1.0))) * (ig * xc)

    row = lax.broadcasted_iota(jnp.int32, (T, tc), 0)
    s = 1
    while s < T:
        keep = row >= s
        u = u + a * jnp.where(keep, pltpu.roll(u, s, 0), 0.0)
        a = a * jnp.where(keep, pltpu.roll(a, s, 0), 1.0)
        s *= 2
    h = u + a * hc_ref[...]
    hc_ref[...] = h[T - 1:T, :]
    y_ref[...] = (h * _gelu(ga_ref[...])).astype(BF16)

    @pl.when(t == nt - 1)
    def _():
        cst_ref[0] = buf_ref[T:T + HIST_ROWS, :]
        hl_ref[0] = h[T - 1:T, :]


def _lru_call(proj, y_prev, hist, h0, p, e, *, n_seq, L, T, row0, m):
    nt = L // T
    rb0 = row0 // T
    tc = LRU_TILE
    nc = LRU_WIDTH // tc
    hpt = tc // LRU_BLOCK
    row = lambda s, c, t: rb0 + s * nt + t
    vec = lambda: pl.BlockSpec((None, 1, tc), lambda s, c, t: (e, 0, c))
    in_specs = [
        pl.BlockSpec((T, tc), lambda s, c, t: (row(s, c, t), c)),
        pl.BlockSpec((T, tc), lambda s, c, t: (row(s, c, t), nc + c)),
        pl.BlockSpec((1, HIST_ROWS, tc), lambda s, c, t: (s, 0, c)),
        pl.BlockSpec((1, 1, tc), lambda s, c, t: (s, 0, c)),
        pl.BlockSpec((None, LRU_CONV, tc), lambda s, c, t: (e, 0, c)),
        vec(),
        pl.BlockSpec((None, hpt, LRU_BLOCK, LRU_BLOCK), lambda s, c, t: (e, c, 0, 0)),
        vec(),
        pl.BlockSpec((None, hpt, LRU_BLOCK, LRU_BLOCK), lambda s, c, t: (e, c, 0, 0)),
        vec(),
        vec(),
    ]
    args = [proj, proj, hist, h0, p["lru_conv_w"], p["lru_conv_b"], p["lru_w_r"], p["lru_b_r"], p["lru_w_i"], p["lru_b_i"],
            p["lru_lambda"]]
    aliases = {}
    if y_prev is not None:
        in_specs.append(pl.BlockSpec(memory_space=pl.ANY))
        args.append(y_prev)
        aliases = {len(args) - 1: 0}
    return pl.pallas_call(
        functools.partial(_lru_kernel, T=T, nt=nt, aliased=y_prev is not None),
        name=f"lru_L{L}",
        out_shape=(jax.ShapeDtypeStruct((m, D_MODEL), BF16),
                   jax.ShapeDtypeStruct((n_seq, HIST_ROWS, LRU_WIDTH), F32),
                   jax.ShapeDtypeStruct((n_seq, 1, LRU_WIDTH), F32)),
        grid=(n_seq, nc, nt),
        in_specs=in_specs,
        out_specs=(pl.BlockSpec((T, tc), lambda s, c, t: (row(s, c, t), c)),
                   pl.BlockSpec((1, HIST_ROWS, tc), lambda s, c, t: (s, 0, c)),
                   pl.BlockSpec((1, 1, tc), lambda s, c, t: (s, 0, c))),
        scratch_shapes=[pltpu.VMEM((T + HIST_ROWS, tc), F32), pltpu.VMEM((1, tc), F32)],
        input_output_aliases=aliases,
        compiler_params=pltpu.CompilerParams(dimension_semantics=("arbitrary", "arbitrary", "arbitrary"),
                                             vmem_limit_bytes=VMEM_LIMIT_SMALL),
    )(*args)


def _ssd_kernel(*refs, T, nt):
    (z_ref, x_ref, b_ref, c_ref, dt_ref, hx_ref, hb_ref, hcn_ref, s0_ref,
     cwx_ref, cbx_ref, cwb_ref, cbb_ref, cwc_ref, cbc_ref, alog_ref, dd_ref, nw_ref, _y_prev,
     y_ref, csx_ref, csb_ref, csc_ref, sf_ref, bufx_ref, bufb_ref, bufc_ref, s_ref, yd_ref) = refs
    t = pl.program_id(2)
    P = SSD_HEAD_DIM
    gw, n = GROUP_WIDTH, SSD_STATE
    width = x_ref.shape[1]

    xs_all = _silu(_conv_step(bufx_ref, x_ref[...], hx_ref, cwx_ref, cbx_ref, SSD_CONV, t, T))
    bm_all = _silu(_conv_step(bufb_ref, b_ref[...], hb_ref, cwb_ref, cbb_ref, SSD_CONV, t, T))
    cm_all = _silu(_conv_step(bufc_ref, c_ref[...], hcn_ref, cwc_ref, cbc_ref, SSD_CONV, t, T))

    @pl.when(t == 0)
    def _():
        s_ref[...] = s0_ref[0]

    dt = dt_ref[...]
    adt = -jnp.exp(alog_ref[...]) * dt
    row = lax.broadcasted_iota(jnp.int32, (T, width), 0)
    acs_all = adt
    s = 1
    while s < T:
        acs_all = acs_all + jnp.where(row >= s, pltpu.roll(acs_all, s, 0), 0.0)
        s *= 2
    xd_all = xs_all * dt
    eacs_all = jnp.exp(acs_all)
    last_all = acs_all[T - 1:T, :]
    xdd_all = xd_all * jnp.exp(last_all - acs_all)
    elast_all = jnp.exp(last_all)
    tri = lax.broadcasted_iota(jnp.int32, (T, T), 0) >= lax.broadcasted_iota(jnp.int32, (T, T), 1)

    G = SSD_GROUPS_PER_STEP
    HPL = LANES // P
    gcols = [slice(gg * gw, (gg + 1) * gw) for gg in range(G)]
    bmb = [bm_all[:, gg * n:(gg + 1) * n].astype(BF16) for gg in range(G)]
    cmb = [cm_all[:, gg * n:(gg + 1) * n].astype(BF16) for gg in range(G)]
    xdb_all = xd_all.astype(BF16)
    cb = [lax.dot_general(cmb[gg], bmb[gg], (((1,), (1,)), ((), ())), preferred_element_type=F32) for gg in range(G)]
    st = [s_ref[gg] for gg in range(G)]
    y_off = [jnp.dot(cmb[gg], st[gg].astype(BF16), preferred_element_type=F32) for gg in range(G)]
    st_inc = [lax.dot_general(bmb[gg], xdd_all[:, gcols[gg]].astype(BF16), (((0,), (0,)), ((), ())),
                              preferred_element_type=F32) for gg in range(G)]
    blk_t = []
    for b in range(width // LANES):
        blk = acs_all[:, b * LANES:(b + 1) * LANES]
        if T < LANES:
            blk = jnp.concatenate([blk, jnp.zeros((LANES - T, LANES), F32)], axis=0)
        blk_t.append(blk.T)
    md = []
    for h in range(width // P):
        col = acs_all[:, h * P:h * P + 1]
        rowv = blk_t[h // HPL][(h % HPL) * P:(h % HPL) * P + 1, :T]
        decay = jnp.exp(jnp.where(tri, col - rowv, -jnp.inf))
        md.append((cb[h * P // gw] * decay).astype(BF16))
    for h in range(width // P):
        yd_ref[:, h * P:(h + 1) * P] = jnp.dot(md[h], xdb_all[:, h * P:(h + 1) * P], preferred_element_type=F32)
    for gg in range(G):
        cols = gcols[gg]
        s_ref[gg] = st[gg] * elast_all[:, cols] + st_inc[gg]
        yt = yd_ref[:, cols] + y_off[gg] * eacs_all[:, cols] + dd_ref[:, cols] * xs_all[:, cols]
        yt = yt * _silu(z_ref[:, cols])
        yt = yt * lax.rsqrt(jnp.mean(yt * yt, axis=-1, keepdims=True) + NORM_EPS) * nw_ref[:, cols]
        y_ref[:, cols] = yt.astype(BF16)

    @pl.when(t == nt - 1)
    def _():
        csx_ref[0] = bufx_ref[T:T + HIST_ROWS, :]
        csb_ref[0] = bufb_ref[T:T + HIST_ROWS, :]
        csc_ref[0] = bufc_ref[T:T + HIST_ROWS, :]
        sf_ref[0] = s_ref[...]


def _ssd_call(proj, dtr, y_prev, hist, s0, p, e, *, n_seq, L, T, row0, m):
    nt = L // T
    rb0 = row0 // T
    gs = SSD_GROUPS_PER_STEP
    G = SSD_GROUPS
    gw = gs * GROUP_WIDTH
    n = gs * SSD_STATE
    row = lambda s, g, t: rb0 + s * nt + t
    zb = OFF_SSD_Z // gw
    xb = OFF_SSD_XBC // gw
    bb = (OFF_SSD_XBC + SSD_INNER) // n
    cb = (OFF_SSD_XBC + SSD_INNER + SSD_BC) // n
    hb = SSD_INNER // n
    hc = (SSD_INNER + SSD_BC) // n
    vec = lambda: pl.BlockSpec((None, 1, gw), lambda s, g, t: (e, 0, g))
    in_specs = [
        pl.BlockSpec((T, gw), lambda s, g, t: (row(s, g, t), zb + g)),
        pl.BlockSpec((T, gw), lambda s, g, t: (row(s, g, t), xb + g)),
        pl.BlockSpec((T, n), lambda s, g, t: (row(s, g, t), bb + g)),
        pl.BlockSpec((T, n), lambda s, g, t: (row(s, g, t), cb + g)),
        pl.BlockSpec((T, gw), lambda s, g, t: (row(s, g, t), g)),
        pl.BlockSpec((1, HIST_ROWS, gw), lambda s, g, t: (s, 0, g)),
        pl.BlockSpec((1, HIST_ROWS, n), lambda s, g, t: (s, 0, hb + g)),
        pl.BlockSpec((1, HIST_ROWS, n), lambda s, g, t: (s, 0, hc + g)),
        pl.BlockSpec((1, gs, SSD_STATE, GROUP_WIDTH), lambda s, g, t: (s, g, 0, 0)),
        pl.BlockSpec((None, SSD_CONV, gw), lambda s, g, t: (e, 0, g)),
        pl.BlockSpec((None, 1, gw), lambda s, g, t: (e, 0, g)),
        pl.BlockSpec((None, SSD_CONV, n), lambda s, g, t: (e, 0, hb + g)),
        pl.BlockSpec((None, 1, n), lambda s, g, t: (e, 0, hb + g)),
        pl.BlockSpec((None, SSD_CONV, n), lambda s, g, t: (e, 0, hc + g)),
        pl.BlockSpec((None, 1, n), lambda s, g, t: (e, 0, hc + g)),
        vec(), vec(), vec(),
        pl.BlockSpec(memory_space=pl.ANY),
    ]
    args = [proj, proj, proj, proj, dtr, hist, hist, hist, s0,
            p["ssd_conv_w"], p["ssd_conv_b"], p["ssd_conv_w"], p["ssd_conv_b"], p["ssd_conv_w"], p["ssd_conv_b"],
            p["ssd_a_log_x"], p["ssd_d_x"], p["ssd_norm_w"], y_prev]
    yb = LRU_WIDTH // gw
    return pl.pallas_call(
        functools.partial(_ssd_kernel, T=T, nt=nt),
        name=f"ssd_L{L}",
        out_shape=(jax.ShapeDtypeStruct((m, D_MODEL), BF16),
                   jax.ShapeDtypeStruct((n_seq, HIST_ROWS, SSD_INNER), F32),
                   jax.ShapeDtypeStruct((n_seq, HIST_ROWS, SSD_BC), F32),
                   jax.ShapeDtypeStruct((n_seq, HIST_ROWS, SSD_BC), F32),
                   jax.ShapeDtypeStruct((n_seq, G, SSD_STATE, GROUP_WIDTH), F32)),
        grid=(n_seq, G // gs, nt),
        in_specs=in_specs,
        out_specs=(pl.BlockSpec((T, gw), lambda s, g, t: (row(s, g, t), yb + g)),
                   pl.BlockSpec((1, HIST_ROWS, gw), lambda s, g, t: (s, 0, g)),
                   pl.BlockSpec((1, HIST_ROWS, n), lambda s, g, t: (s, 0, g)),
                   pl.BlockSpec((1, HIST_ROWS, n), lambda s, g, t: (s, 0, g)),
                   pl.BlockSpec((1, gs, SSD_STATE, GROUP_WIDTH), lambda s, g, t: (s, g, 0, 0))),
        scratch_shapes=[pltpu.VMEM((T + HIST_ROWS, gw), F32), pltpu.VMEM((T + HIST_ROWS, n), F32),
                        pltpu.VMEM((T + HIST_ROWS, n), F32), pltpu.VMEM((gs, SSD_STATE, GROUP_WIDTH), F32),
                        pltpu.VMEM((T, gw), F32)],
        input_output_aliases={len(args) - 1: 0},
        compiler_params=pltpu.CompilerParams(dimension_semantics=("arbitrary", "arbitrary", "arbitrary"),
                                             vmem_limit_bytes=VMEM_LIMIT_SMALL),
    )(*args)


def _ret_kernel(*refs, nt, n_aliased):
    (q_ref, k_ref, v_ref, g_ref, cos_ref, sin_ref, dm_ref, cr_ref, kd_ref, cd_ref, s0_ref) = refs[:11]
    y_ref, sf_ref, s_ref = refs[11 + n_aliased:]
    t = pl.program_id(2)
    half = RET_QK_DIM // 2
    dk, dv = RET_QK_DIM, RET_V_DIM

    @pl.when(t == 0)
    def _():
        s_ref[...] = s0_ref[0]

    cos = cos_ref[...]
    sin = sin_ref[...]

    def rot(x):
        x1 = x[:, :half]
        x2 = x[:, half:]
        return jnp.concatenate([x1 * cos - x2 * sin, x2 * cos + x1 * sin], axis=-1)

    for hh in range(RET_HEADS_PER_STEP):
        q = rot(q_ref[:, hh * dk:(hh + 1) * dk].astype(F32))
        k = rot(k_ref[:, hh * dk:(hh + 1) * dk].astype(F32)) * (RET_QK_DIM ** -0.5)
        qb = q.astype(BF16)
        vb = v_ref[:, hh * dv:(hh + 1) * dv].astype(BF16)
        att = lax.dot_general(qb, k.astype(BF16), (((1,), (1,)), ((), ())), preferred_element_type=F32) * dm_ref[hh]
        st = s_ref[hh]
        o = (jnp.dot(att.astype(BF16), vb, preferred_element_type=F32)
             + jnp.dot((q * cr_ref[hh]).astype(BF16), st.astype(BF16), preferred_element_type=F32))
        st_new = st * cd_ref[hh][0:1, :] + lax.dot_general((k * kd_ref[hh]).astype(BF16), vb, (((0,), (0,)), ((), ())),
                                                            preferred_element_type=F32)
        s_ref[hh] = st_new
        o = o * lax.rsqrt(jnp.mean(o * o, axis=-1, keepdims=True) + NORM_EPS)
        y_ref[:, hh * dv:(hh + 1) * dv] = (_silu(g_ref[:, hh * dv:(hh + 1) * dv].astype(F32)) * o).astype(BF16)

    @pl.when(t == nt - 1)
    def _():
        sf_ref[0] = s_ref[...]


def _ret_tables(T, ct, pos0, L):
    f32 = F32
    log_gamma = jnp.log1p(-jnp.exp2(-5.0 - jnp.arange(RET_HEADS, dtype=f32)))
    n = jnp.arange(T, dtype=f32)
    chunk = jnp.arange(T) // ct
    same = chunk[:, None] == chunk[None, :]
    earlier = chunk[None, :] < chunk[:, None]
    diff = n[:, None] - n[None, :]
    lg = log_gamma[:, None, None]
    dm = jnp.where(same[None], jnp.exp(jnp.abs(diff)[None] * lg), jnp.where(earlier[None], jnp.exp(diff[None] * lg), 0.0))
    cross = jnp.exp((n[None, :] + 1.0) * log_gamma[:, None])
    kvd = jnp.exp((T - 1.0 - n)[None, :] * log_gamma[:, None])
    cdec = jnp.exp(T * log_gamma)
    cr = jnp.broadcast_to(cross[:, :, None], (RET_HEADS, T, RET_QK_DIM))
    kd = jnp.broadcast_to(kvd[:, :, None], (RET_HEADS, T, RET_QK_DIM))
    cd = jnp.broadcast_to(cdec[:, None, None], (RET_HEADS, SUBLANES, RET_V_DIM))
    half = RET_QK_DIM // 2
    inv = 1.0 / (ROPE_BASE ** jnp.linspace(0.0, 1.0, half, dtype=f32))
    pos = float(pos0) + jnp.arange(L, dtype=f32)
    ang = pos[:, None] * inv[None, :]
    return dm, cr, kd, cd, jnp.cos(ang), jnp.sin(ang)


def _ret_call(proj, y_prev, s0, e0, sf_prev, e, n_layers, *, n_seq, L, T, row0, pos0, m):
    nt = L // T
    rb0 = row0 // T
    ct = min(CHUNK, L)
    assert T % ct == 0
    dm, cr, kd, cd, cos, sin = _ret_tables(T, ct, pos0, L)
    H = RET_HEADS
    hs = RET_HEADS_PER_STEP
    dk, dv = hs * RET_QK_DIM, hs * RET_V_DIM
    row = lambda s, h, t: rb0 + s * nt + t
    kb0 = RET_QK_WIDTH // dk
    vb0 = 2 * RET_QK_WIDTH // dv
    gb0 = (2 * RET_QK_WIDTH + RET_V_WIDTH) // dv
    in_specs = [
        pl.BlockSpec((T, dk), lambda s, h, t: (row(s, h, t), h)),
        pl.BlockSpec((T, dk), lambda s, h, t: (row(s, h, t), kb0 + h)),
        pl.BlockSpec((T, dv), lambda s, h, t: (row(s, h, t), vb0 + h)),
        pl.BlockSpec((T, dv), lambda s, h, t: (row(s, h, t), gb0 + h)),
        pl.BlockSpec((T, RET_QK_DIM // 2), lambda s, h, t: (t, 0)),
        pl.BlockSpec((T, RET_QK_DIM // 2), lambda s, h, t: (t, 0)),
        pl.BlockSpec((hs, T, T), lambda s, h, t: (h, 0, 0)),
        pl.BlockSpec((hs, T, RET_QK_DIM), lambda s, h, t: (h, 0, 0)),
        pl.BlockSpec((hs, T, RET_QK_DIM), lambda s, h, t: (h, 0, 0)),
        pl.BlockSpec((hs, SUBLANES, RET_V_DIM), lambda s, h, t: (h, 0, 0)),
        pl.BlockSpec((None, 1, hs, RET_QK_DIM, RET_V_DIM), lambda s, h, t: (e0, s, h, 0, 0)),
    ]
    args = [proj, proj, proj, proj, cos, sin, dm, cr, kd, cd, s0]
    aliases = {}
    for out_idx, prev in ((0, y_prev), (1, sf_prev)):
        if prev is not None:
            in_specs.append(pl.BlockSpec(memory_space=pl.ANY))
            args.append(prev)
            aliases[len(args) - 1] = out_idx
    return pl.pallas_call(
        functools.partial(_ret_kernel, nt=nt, n_aliased=len(aliases)),
        name=f"ret_L{L}",
        out_shape=(jax.ShapeDtypeStruct((m, RET_V_WIDTH), BF16),
                   jax.ShapeDtypeStruct((n_layers, n_seq, H, RET_QK_DIM, RET_V_DIM), F32)),
        grid=(n_seq, H // hs, nt),
        in_specs=in_specs,
        out_specs=(pl.BlockSpec((T, dv), lambda s, h, t: (row(s, h, t), h)),
                   pl.BlockSpec((None, 1, hs, RET_QK_DIM, RET_V_DIM), lambda s, h, t: (e, s, h, 0, 0))),
        scratch_shapes=[pltpu.VMEM((hs, RET_QK_DIM, RET_V_DIM), F32)],
        input_output_aliases=aliases,
        compiler_params=pltpu.CompilerParams(dimension_semantics=("arbitrary", "arbitrary", "arbitrary"),
                                             vmem_limit_bytes=VMEM_LIMIT_SMALL),
    )(*args)


def _pad_hist(state):
    return jnp.pad(state, ((0, 0), (HIST_ROWS - state.shape[1], 0), (0, 0)))


def _ssd_state_in(s):
    n = s.shape[0]
    s = s.reshape(n, SSD_GROUPS, HEADS_PER_GROUP, SSD_HEAD_DIM, SSD_STATE)
    return s.transpose(0, 1, 4, 2, 3).reshape(n, SSD_GROUPS, SSD_STATE, GROUP_WIDTH)


def _ssd_state_out(s):
    n = s.shape[0]
    s = s.reshape(n, SSD_GROUPS, SSD_STATE, HEADS_PER_GROUP, SSD_HEAD_DIM)
    return s.transpose(0, 1, 3, 4, 2).reshape(n, SSD_HEADS, SSD_HEAD_DIM, SSD_STATE)


def kernel(x_prompt, x_sample, state_lru_conv, state_lru_h, state_ssd_conv, state_ssd, state_ret, state_ffn_conv,
           norm_mix_pre, norm_mix_post, norm_ffn_pre, norm_ffn_post, w_in_ab, lru_conv_w, lru_conv_b, lru_w_r, lru_b_r,
           lru_w_i, lru_b_i, lru_lambda, ssd_conv_w, ssd_conv_b, ssd_dt_bias, ssd_a_log, ssd_d, ssd_norm_w, w_out_ab,
           w_in_ret, w_out_ret, ffn_w_up, ffn_conv_w, ffn_conv_b, ffn_w_down):
    bp, lp, d = x_prompt.shape
    bs, ls, _ = x_sample.shape
    depth = norm_mix_pre.shape[0]
    n_even = w_in_ab.shape[0]
    mp = bp * lp
    m = mp + bs * ls
    assert d == D_MODEL and x_sample.shape[2] == D_MODEL and n_even == (depth + 1) // 2
    assert lp % CHUNK == 0 and ls <= CHUNK, "prompt length in whole chunks, sample length within one chunk"

    groups = ((bp, lp, 0, 0), (bs, ls, mp, PAST_LEN))
    t_lru = (min(256, lp), ls)
    t_ssd = (min(128, lp), ls)
    t_ret = (min(256, lp), ls)

    rep = lambda v: jnp.repeat(v, SSD_HEAD_DIM, axis=-1).reshape(n_even, 1, SSD_INNER)
    vec3 = lambda v: v.reshape(v.shape[0], 1, v.shape[-1])
    ab = {
        "lru_conv_w": lru_conv_w, "lru_conv_b": vec3(lru_conv_b), "lru_w_r": lru_w_r, "lru_b_r": vec3(lru_b_r),
        "lru_w_i": lru_w_i, "lru_b_i": vec3(lru_b_i), "lru_lambda": vec3(lru_lambda),
        "ssd_conv_w": ssd_conv_w, "ssd_conv_b": vec3(ssd_conv_b),
        "ssd_a_log_x": rep(ssd_a_log), "ssd_d_x": rep(ssd_d),
        "ssd_norm_w": vec3(ssd_norm_w),
    }
    ffn_hist = _pad_hist(state_ffn_conv.reshape(depth * bs, FFN_CONV - 1, D_FF)).reshape(depth, bs, HIST_ROWS, D_FF)
    ffn_hist = jnp.pad(ffn_hist, ((0, 0), (bp, 0), (0, 0), (0, 0)))

    new = {k: ([], []) for k in ("lc", "lh", "sc", "ss", "fc")}
    ret_stack = [None, None]
    w_in_ab_t = jnp.swapaxes(w_in_ab, 1, 2)

    x, hn, dtr = _first_norm(x_prompt.reshape(mp, d), x_sample.reshape(bs * ls, d), norm_mix_pre[0], w_in_ab_t, ssd_dt_bias)
    for layer in range(depth):
        e = layer // 2
        if layer % 2 == 0:
            proj = _matmul(hn, w_in_ab_t, e, n_cols=OFF_SSD_DT, w_transposed=True)
            y = None
            for gi, (n_seq, L, row0, _) in enumerate(groups):
                if gi == 0:
                    hist = jnp.zeros((n_seq, HIST_ROWS, LRU_WIDTH), F32)
                    h0 = jnp.zeros((n_seq, 1, LRU_WIDTH), F32)
                else:
                    hist = _pad_hist(state_lru_conv[e])
                    h0 = state_lru_h[e].reshape(n_seq, 1, LRU_WIDTH)
                y, cst, hl = _lru_call(proj, y, hist, h0, ab, e, n_seq=n_seq, L=L, T=t_lru[gi], row0=row0, m=m)
                new["lc"][gi].append(cst[:, HIST_ROWS - (LRU_CONV - 1):, :])
                new["lh"][gi].append(hl.reshape(n_seq, LRU_WIDTH))
            for gi, (n_seq, L, row0, _) in enumerate(groups):
                if gi == 0:
                    hist = jnp.zeros((n_seq, HIST_ROWS, SSD_CONV_DIM), F32)
                    s0 = jnp.zeros((n_seq, SSD_GROUPS, SSD_STATE, GROUP_WIDTH), F32)
                else:
                    hist = _pad_hist(state_ssd_conv[e])
                    s0 = _ssd_state_in(state_ssd[e])
                y, csx, csb, csc, sf = _ssd_call(proj, dtr, y, hist, s0, ab, e, n_seq=n_seq, L=L, T=t_ssd[gi], row0=row0, m=m)
                cst = jnp.concatenate([csx, csb, csc], axis=-1)
                new["sc"][gi].append(cst[:, HIST_ROWS - (SSD_CONV - 1):, :])
                new["ss"][gi].append(_ssd_state_out(sf))
            mix = _matmul(y, w_out_ab, e, out_dtype=BF16)
        else:
            proj = _matmul(hn, w_in_ret, e, out_dtype=BF16)
            y = None
            for gi, (n_seq, L, row0, pos0) in enumerate(groups):
                if gi == 0:
                    s0, e0 = jnp.zeros((1, n_seq, RET_HEADS, RET_QK_DIM, RET_V_DIM), F32), 0
                else:
                    s0, e0 = state_ret, e
                y, ret_stack[gi] = _ret_call(proj, y, s0, e0, ret_stack[gi], e, depth // 2, n_seq=n_seq, L=L, T=t_ret[gi],
                                             row0=row0, pos0=pos0, m=m)
            mix = _matmul(y, w_out_ret, e, out_dtype=BF16)
        x, hn = _resid_norm(x, mix, norm_mix_post[layer], norm_ffn_pre[layer])
        h, fst = _ffn_up(hn, ffn_w_up, ffn_conv_w, ffn_conv_b, ffn_hist[layer], layer, bp=bp, lp=lp, bs=bs, ls=ls)
        fst = fst[:, HIST_ROWS - (FFN_CONV - 1):, :]
        new["fc"][0].append(fst[:bp])
        new["fc"][1].append(fst[bp:])
        f = _matmul(h, ffn_w_down, layer, out_dtype=BF16)
        if layer + 1 < depth:
            if (layer + 1) % 2 == 0:
                x, hn, dtr = _resid_norm(x, f, norm_ffn_post[layer], norm_mix_pre[layer + 1],
                                         dt_of=(w_in_ab_t, ssd_dt_bias, (layer + 1) // 2))
            else:
                x, hn = _resid_norm(x, f, norm_ffn_post[layer], norm_mix_pre[layer + 1])
        else:
            y_prompt, y_sample = _last_resid(x, f, norm_ffn_post[layer], mp)

    outs = [y_prompt.reshape(bp, lp, d), y_sample.reshape(bs, ls, d)]
    for gi in range(2):
        for k in ("lc", "lh", "sc", "ss", "rs", "fc"):
            outs.append(ret_stack[gi] if k == "rs" else jnp.stack(new[k][gi]))
    return tuple(outs)
```
